```python
import jax, jax.numpy as jnp
from jax import lax
import numpy as np


D_MODEL = 1024
BATCH = 4
SEQ = 8192
DEPTH = 1

HEAD_DIM = 64
ROPE_THETA = 10000.0
EPS = 1e-6
ATTN_SCALE = HEAD_DIM ** -0.5
Q_BLOCK = 128

DSA_HEADS = 8
DSA_Q_RANK = 256
DSA_KV_RANK = 128
IDX_HEADS = 8
IDX_DIM = 64
IDX_SCALE = IDX_HEADS ** -0.5 * IDX_DIM ** -0.5
DSA_TOPK_MAX = 256

NSA_HEADS = 8
NSA_KV_HEADS = 2
NSA_GROUP = NSA_HEADS // NSA_KV_HEADS
CMP_BLOCK = 32
CMP_STRIDE = 16
SLC_BLOCK = 64
SLC_COUNT = 16
WINDOW = 512
FORCE_BONUS = 1e4

MIX_WIDTH = DSA_HEADS * HEAD_DIM + NSA_HEADS * HEAD_DIM
IN_SIZES = (DSA_Q_RANK, DSA_KV_RANK, IDX_DIM, IDX_HEADS,
            NSA_HEADS * HEAD_DIM, 3 * 2 * NSA_KV_HEADS * HEAD_DIM, NSA_HEADS * 3)
IN_WIDTH = sum(IN_SIZES)

MEM_LEN = 256
XATTN_HEADS = 4
XATTN_HEAD_DIM = 128

D_FF = 4 * D_MODEL

kernel_name = "hybrid_dsa_nsa_memory_block"


def _rms_norm(x, g):
    xf = x.astype(jnp.float32)
    y = xf * lax.rsqrt(jnp.mean(xf * xf, axis=-1, keepdims=True) + EPS)
    return (y * g.astype(jnp.float32)).astype(x.dtype)


def _layer_norm(x, g, b):
    xf = x.astype(jnp.float32)
    mu = jnp.mean(xf, axis=-1, keepdims=True)
    var = jnp.mean(jnp.square(xf - mu), axis=-1, keepdims=True)
    y = (xf - mu) * lax.rsqrt(var + EPS) * g.astype(jnp.float32) + b.astype(jnp.float32)
    return y.astype(x.dtype)


def _rope(x, pos):
    half = x.shape[-1] // 2
    inv = ROPE_THETA ** (-jnp.arange(half, dtype=jnp.float32) / half)
    ang = pos.astype(jnp.float32)[:, None] * inv[None, :]
    cos = jnp.cos(ang)[:, None, :]
    sin = jnp.sin(ang)[:, None, :]
    xf = x.astype(jnp.float32)
    x1, x2 = xf[..., :half], xf[..., half:]
    return jnp.concatenate([x1 * cos - x2 * sin, x1 * sin + x2 * cos], axis=-1).astype(x.dtype)


def _masked_softmax(s, mask):
    s = jnp.where(mask, s, -jnp.inf)
    m = jnp.max(s, axis=-1, keepdims=True)
    m = jnp.where(jnp.isfinite(m), m, 0.0)
    p = jnp.where(mask, jnp.exp(s - m), 0.0)
    return p / jnp.maximum(jnp.sum(p, axis=-1, keepdims=True), 1e-30)


def _hybrid_mixer(h, w_in, dsa_cq_g, dsa_ckv_g, w_dsa_uq, w_dsa_ukv, w_idx_q, idx_k_ln_g, idx_k_ln_b,
                  dsa_qn_g, dsa_kn_g, nsa_cmp_pe, w_nsa_cmp, nsa_qn_g, nsa_kn_g, w_out):
    B, L, _ = h.shape
    G, R, D = NSA_KV_HEADS, NSA_GROUP, HEAD_DIM
    pos = jnp.arange(L, dtype=jnp.int32)
    splits = [int(v) for v in np.cumsum(IN_SIZES)[:-1]]
    proj = h @ w_in
    c_q, c_kv, k_idx, w_idx, q_nsa, kv_nsa, g_nsa = jnp.split(proj, splits, axis=-1)

    c_q = _rms_norm(c_q, dsa_cq_g)
    c_kv = _rms_norm(c_kv, dsa_ckv_g)
    q_d = _rope(_rms_norm((c_q @ w_dsa_uq).reshape(B, L, DSA_HEADS, D), dsa_qn_g), pos)
    k_d, v_d = jnp.split(c_kv @ w_dsa_ukv, 2, axis=-1)
    k_d = _rope(_rms_norm(k_d, dsa_kn_g)[:, :, None, :], pos)[:, :, 0, :]
    q_i = _rope((c_q @ w_idx_q).reshape(B, L, IDX_HEADS, IDX_DIM), pos)
    k_i = _rope(_layer_norm(k_idx, idx_k_ln_g, idx_k_ln_b)[:, :, None, :], pos)[:, :, 0, :]
    w_i = w_idx * IDX_SCALE
    k_dsa = min(DSA_TOPK_MAX, L // 4)

    q_n = _rope(_rms_norm(q_nsa.reshape(B, L, NSA_HEADS, D), nsa_qn_g), pos)
    kv = kv_nsa.reshape(B, L, 3, 2, G, D)
    k_br, v_br = kv[:, :, :, 0], kv[:, :, :, 1]
    k_slc = _rope(_rms_norm(k_br[:, :, 1], nsa_kn_g[1]), pos)
    k_win = _rope(_rms_norm(k_br[:, :, 2], nsa_kn_g[2]), pos)
    v_slc, v_win = v_br[:, :, 1], v_br[:, :, 2]

    n_cmp = (L - CMP_BLOCK) // CMP_STRIDE + 1
    tok = jnp.arange(n_cmp)[:, None] * CMP_STRIDE + jnp.arange(CMP_BLOCK)[None, :]
    cmp_end = tok[:, -1]
    k_blocks = k_br[:, :, 0][:, tok] + nsa_cmp_pe[0][:, None, :]
    v_blocks = v_br[:, :, 0][:, tok] + nsa_cmp_pe[1][:, None, :]
    k_c = jnp.einsum('bnlgd,lde->bnge', k_blocks, w_nsa_cmp[0])
    v_c = jnp.einsum('bnlgd,lde->bnge', v_blocks, w_nsa_cmp[1])
    k_c = _rope(_rms_norm(k_c, nsa_kn_g[0]), cmp_end)

    n_slc = L // SLC_BLOCK
    n_sel = min(SLC_COUNT, n_slc)
    ci = jnp.arange(n_cmp)[:, None] * CMP_STRIDE
    sj = jnp.arange(n_slc)[None, :] * SLC_BLOCK
    overlap = ((ci < sj + SLC_BLOCK) & (ci + CMP_BLOCK > sj)).astype(jnp.float32)
    k_slc_blk = k_slc.reshape(B, n_slc, SLC_BLOCK, G, D).transpose(0, 3, 1, 2, 4)
    v_slc_blk = v_slc.reshape(B, n_slc, SLC_BLOCK, G, D).transpose(0, 3, 1, 2, 4)

    k_win_pad = jnp.pad(k_win, ((0, 0), (WINDOW, 0), (0, 0), (0, 0)))
    v_win_pad = jnp.pad(v_win, ((0, 0), (WINDOW, 0), (0, 0), (0, 0)))

    gates = jax.nn.sigmoid(g_nsa.astype(jnp.float32)).astype(h.dtype).reshape(B, L, NSA_HEADS, 3)

    n_qb = L // Q_BLOCK

    def blocks(a):
        return jnp.moveaxis(a.reshape((B, n_qb, Q_BLOCK) + a.shape[2:]), 1, 0)

    bidx3 = jnp.arange(B)[:, None, None]
    bidx4 = jnp.arange(B)[:, None, None, None]
    gidx4 = jnp.arange(G)[None, :, None, None]
    j_slc = jnp.arange(n_slc)

    def step(args):
        qb, qd, qi, wi, qn, gt = args
        t = qb * Q_BLOCK + jnp.arange(Q_BLOCK, dtype=jnp.int32)

        idx_score = jnp.einsum('bqhs,bqh->bqs',
                               jax.nn.relu(jnp.einsum('bqhd,bsd->bqhs', qi, k_i)), wi).astype(jnp.float32)
        idx_score = jnp.where(pos[None, None, :] <= t[None, :, None], idx_score, -jnp.inf)
        _, sel = lax.top_k(idx_score, k_dsa)
        kg = k_d[bidx3, sel]
        vg = v_d[bidx3, sel]
        s = jnp.einsum('bqhd,bqkd->bhqk', qd, kg).astype(jnp.float32) * ATTN_SCALE
        p = _masked_softmax(s, (sel <= t[None, :, None])[:, None])
        o_dsa = jnp.einsum('bhqk,bqkd->bqhd', p.astype(vg.dtype), vg)

        qg = qn.reshape(B, Q_BLOCK, G, R, D)
        s = jnp.einsum('bqgrd,bngd->bgrqn', qg, k_c).astype(jnp.float32) * ATTN_SCALE
        p_cmp = _masked_softmax(s, cmp_end[None, :] <= t[:, None])
        o_cmp = jnp.einsum('bgrqn,bngd->bqgrd', p_cmp.astype(v_c.dtype), v_c)
        blk_score = jnp.einsum('bgrqn,nj->bgqj', p_cmp, overlap)
        tb = t // SLC_BLOCK
        forced = (j_slc[None, :] == 0) | (j_slc[None, :] == tb[:, None]) | (j_slc[None, :] == tb[:, None] - 1)
        admissible = j_slc[None, :] * SLC_BLOCK <= t[:, None]
        blk_score = jnp.where(admissible, blk_score + jnp.where(forced, FORCE_BONUS, 0.0), -jnp.inf)
        _, sel_blk = lax.top_k(blk_score, n_sel)
        kb = k_slc_blk[bidx4, gidx4, sel_blk].reshape(B, G, Q_BLOCK, n_sel * SLC_BLOCK, D)
        vb = v_slc_blk[bidx4, gidx4, sel_blk].reshape(B, G, Q_BLOCK, n_sel * SLC_BLOCK, D)
        kpos = (sel_blk[..., None] * SLC_BLOCK + jnp.arange(SLC_BLOCK)).reshape(B, G, Q_BLOCK, n_sel * SLC_BLOCK)
        s = jnp.einsum('bqgrd,bgqkd->bgrqk', qg, kb).astype(jnp.float32) * ATTN_SCALE
        p = _masked_softmax(s, (kpos <= t[None, None, :, None])[:, :, None])
        o_slc = jnp.einsum('bgrqk,bgqkd->bqgrd', p.astype(vb.dtype), vb)
        kw = lax.dynamic_slice_in_dim(k_win_pad, qb * Q_BLOCK, WINDOW + Q_BLOCK, axis=1)
        vw = lax.dynamic_slice_in_dim(v_win_pad, qb * Q_BLOCK, WINDOW + Q_BLOCK, axis=1)
        wpos = qb * Q_BLOCK - WINDOW + jnp.arange(WINDOW + Q_BLOCK, dtype=jnp.int32)
        wmask = (wpos[None, :] <= t[:, None]) & (wpos[None, :] > t[:, None] - WINDOW) & (wpos[None, :] >= 0)
        s = jnp.einsum('bqgrd,bkgd->bgrqk', qg, kw).astype(jnp.float32) * ATTN_SCALE
        p = _masked_softmax(s, wmask)
        o_win = jnp.einsum('bgrqk,bkgd->bqgrd', p.astype(vw.dtype), vw)
        gt = gt.reshape(B, Q_BLOCK, G, R, 3)
        o_nsa = o_cmp * gt[..., 0:1] + o_slc * gt[..., 1:2] + o_win * gt[..., 2:3]

        return jnp.concatenate([o_dsa.reshape(B, Q_BLOCK, -1), o_nsa.reshape(B, Q_BLOCK, -1)], axis=-1)

    xs = (jnp.arange(n_qb, dtype=jnp.int32), blocks(q_d), blocks(q_i), blocks(w_i), blocks(q_n), blocks(gates))
    o = lax.map(step, xs)
    o = jnp.moveaxis(o, 0, 1).reshape(B, L, MIX_WIDTH)
    return o @ w_out


def _memory_xattn(h, mem, mem_norm_g, w_xq, w_xk, w_xv, xq_norm_g, xk_norm_g, w_xo):
    B, L, _ = h.shape
    M = mem.shape[1]
    m = _rms_norm(mem, mem_norm_g)
    q = _rms_norm((h @ w_xq).reshape(B, L, XATTN_HEADS, XATTN_HEAD_DIM), xq_norm_g)
    k = _rms_norm((m @ w_xk).reshape(B, M, XATTN_HEADS, XATTN_HEAD_DIM), xk_norm_g)
    v = (m @ w_xv).reshape(B, M, XATTN_HEADS, XATTN_HEAD_DIM)
    s = jnp.einsum('bqhd,bmhd->bhqm', q, k).astype(jnp.float32) * (XATTN_HEAD_DIM ** -0.5)
    p = jax.nn.softmax(s, axis=-1).astype(v.dtype)
    o = jnp.einsum('bhqm,bmhd->bqhd', p, v).reshape(B, L, XATTN_HEADS * XATTN_HEAD_DIM)
    return o @ w_xo


def _sq_relu_mlp(h, w_ff_in, w_ff_out):
    return jnp.square(jax.nn.relu(h @ w_ff_in)) @ w_ff_out


def setup_inputs(seed: int = 0) -> dict:
    key = jax.random.key(seed)
    D = HEAD_DIM
    specs = [
        ("norm1_g", (D_MODEL,), "g", 1),
        ("w_in", (D_MODEL, IN_WIDTH), "w", D_MODEL),
        ("dsa_cq_g", (DSA_Q_RANK,), "g", 1),
        ("dsa_ckv_g", (DSA_KV_RANK,), "g", 1),
        ("w_dsa_uq", (DSA_Q_RANK, DSA_HEADS * D), "w", DSA_Q_RANK),
        ("w_dsa_ukv", (DSA_KV_RANK, 2 * D), "w", DSA_KV_RANK),
        ("w_idx_q", (DSA_Q_RANK, IDX_HEADS * IDX_DIM), "w", DSA_Q_RANK),
        ("idx_k_ln_g", (IDX_DIM,), "g", 1),
        ("idx_k_ln_b", (IDX_DIM,), "b", 1),
        ("dsa_qn_g", (D,), "g", 1),
        ("dsa_kn_g", (D,), "g", 1),
        ("nsa_cmp_pe", (2, CMP_BLOCK, D), "b", 1),
        ("w_nsa_cmp", (2, CMP_BLOCK, D, D), "w", CMP_BLOCK * D),
        ("nsa_qn_g", (D,), "g", 1),
        ("nsa_kn_g", (3, D), "g", 1),
        ("w_out", (MIX_WIDTH, D_MODEL), "w", MIX_WIDTH),
        ("norm2_g", (D_MODEL,), "g", 1),
        ("mem_norm_g", (D_MODEL,), "g", 1),
        ("w_xq", (D_MODEL, XATTN_HEADS * XATTN_HEAD_DIM), "w", D_MODEL),
        ("w_xk", (D_MODEL, XATTN_HEADS * XATTN_HEAD_DIM), "w", D_MODEL),
        ("w_xv", (D_MODEL, XATTN_HEADS * XATTN_HEAD_DIM), "w", D_MODEL),
        ("xq_norm_g", (XATTN_HEAD_DIM,), "g", 1),
        ("xk_norm_g", (XATTN_HEAD_DIM,), "g", 1),
        ("w_xo", (XATTN_HEADS * XATTN_HEAD_DIM, D_MODEL), "w", XATTN_HEADS * XATTN_HEAD_DIM),
        ("norm3_g", (D_MODEL,), "g", 1),
        ("w_ff_in", (D_MODEL, D_FF), "w", D_MODEL),
        ("w_ff_out", (D_FF, D_MODEL), "w", D_FF),
    ]
    keys = jax.random.split(key, len(specs) + 2)
    out = {
        "x": jax.random.normal(keys[0], (BATCH, SEQ, D_MODEL), jnp.float32),
        "mem": jax.random.normal(keys[1], (BATCH, MEM_LEN, D_MODEL), jnp.float32),
    }
    for k, (name, shape, kind, fan) in zip(keys[2:], specs):
        z = jax.random.normal(k, (DEPTH,) + shape, jnp.float32)
        if kind == "w":
            out[name] = z * (fan ** -0.5)
        elif kind == "g":
            out[name] = 1.0 + 0.02 * z
        else:
            out[name] = 0.02 * z
    return out


def reference(x, mem, norm1_g, w_in, dsa_cq_g, dsa_ckv_g, w_dsa_uq, w_dsa_ukv, w_idx_q, idx_k_ln_g, idx_k_ln_b,
              dsa_qn_g, dsa_kn_g, nsa_cmp_pe, w_nsa_cmp, nsa_qn_g, nsa_kn_g, w_out, norm2_g, mem_norm_g,
              w_xq, w_xk, w_xv, xq_norm_g, xk_norm_g, w_xo, norm3_g, w_ff_in, w_ff_out):
    for l in range(DEPTH):
        x = x + _hybrid_mixer(_rms_norm(x, norm1_g[l]), w_in[l], dsa_cq_g[l], dsa_ckv_g[l], w_dsa_uq[l],
                              w_dsa_ukv[l], w_idx_q[l], idx_k_ln_g[l], idx_k_ln_b[l], dsa_qn_g[l], dsa_kn_g[l],
                              nsa_cmp_pe[l], w_nsa_cmp[l], nsa_qn_g[l], nsa_kn_g[l], w_out[l])
        x = x + _memory_xattn(_rms_norm(x, norm2_g[l]), mem, mem_norm_g[l], w_xq[l], w_xk[l], w_xv[l],
                              xq_norm_g[l], xk_norm_g[l], w_xo[l])
        x = x + _sq_relu_mlp(_rms_norm(x, norm3_g[l]), w_ff_in[l], w_ff_out[l])
    return x
```

```python
import functools

import jax
import jax.numpy as jnp
import numpy as np
from jax import lax
from jax.experimental import pallas as pl
from jax.experimental.pallas import tpu as pltpu

F32 = jnp.float32
I32 = jnp.int32
_MM_DTYPE = jnp.bfloat16

HEAD_DIM = 64
HALF_DIM = HEAD_DIM // 2
ROPE_THETA = 10000.0
EPS = 1e-6
ATTN_SCALE = HEAD_DIM ** -0.5

DSA_HEADS = 8
DSA_Q_RANK = 256
DSA_KV_RANK = 128
IDX_HEADS = 8
IDX_DIM = 64
IDX_SCALE = IDX_HEADS ** -0.5 * IDX_DIM ** -0.5
DSA_TOPK_MAX = 256

NSA_HEADS = 8
NSA_KV_HEADS = 2
NSA_GROUP = NSA_HEADS // NSA_KV_HEADS
CMP_BLOCK = 32
CMP_STRIDE = 16
SLC_BLOCK = 64
SLC_COUNT = 16
WINDOW = 512
FORCE_BONUS = 1e4

XATTN_HEADS = 4
XATTN_HEAD_DIM = 128

LANES = 128
SUBLANES = 8
INT_MIN = -(2 ** 31)
NEG_BIG = -1e30
VMEM_LIMIT = 56 * 1024 * 1024

ROW_TILE = 256
MLP_ROW_TILE = 1024
MLP_FF_TILE = 1024
Q_TILE = 128
DSA_KEY_TILE = 256
SLC_KEY_TILE = 512
SELECT_CHUNK = 64


def _mm(a, b):
    return jnp.dot(a.astype(_MM_DTYPE), b.astype(_MM_DTYPE), preferred_element_type=F32)


def _mm_nt(a, b):
    return lax.dot_general(a.astype(_MM_DTYPE), b.astype(_MM_DTYPE), (((1,), (1,)), ((), ())),
                           preferred_element_type=F32)


def _split_mm(y, mat):
    hi = y.astype(_MM_DTYPE)
    lo = (y - hi.astype(F32)).astype(_MM_DTYPE)
    return jnp.dot(hi, mat, preferred_element_type=F32) + jnp.dot(lo, mat, preferred_element_type=F32)


def _rms_rows(x, g):
    return x * lax.rsqrt(jnp.mean(x * x, axis=-1, keepdims=True) + EPS) * g


def _head_rms(x, g, seg):
    return x * lax.rsqrt(_split_mm(x * x, seg) + EPS) * g


def _tile_lanes(t, n):
    return t if n == 1 else jnp.concatenate([t] * n, axis=1)


def _rope(y, cos_t, sin_t):
    w = y.shape[-1]
    lane = lax.broadcasted_iota(I32, y.shape, 1)
    swapped = jnp.where((lane & HALF_DIM) == 0, pltpu.roll(y, w - HALF_DIM, 1), pltpu.roll(y, HALF_DIM, 1))
    n = w // LANES
    return y * _tile_lanes(cos_t, n) + swapped * _tile_lanes(sin_t, n)


def _memkv_kernel(mem_ref, g_ref, wk_ref, wv_ref, kg_ref, k_ref, v_ref):
    m = _rms_rows(mem_ref[0], g_ref[...])
    k = _mm(m, wk_ref[...])
    v = _mm(m, wv_ref[...])
    ks = []
    for h in range(XATTN_HEADS):
        ks.append(_rms_rows(k[:, h * XATTN_HEAD_DIM:(h + 1) * XATTN_HEAD_DIM], kg_ref[...]))
    k_ref[0] = jnp.concatenate(ks, axis=1).astype(k_ref.dtype)
    v_ref[0] = v.astype(v_ref.dtype)


def _prologue_kernel(x_ref, g1_ref, win_ref, cqg_ref, ckvg_ref, wuq_ref, wukv_ref, widx_ref,
                     lng_ref, lnb_ref, qng_ref, kng_ref, nqg_ref, nkg_ref, seg_ref, cos_ref, sin_ref,
                     qd_ref, qi_ref, qn_ref, kk_ref, vd_ref, misc_ref, kc_ref, vc_ref,
                     ksl_ref, vsl_ref, kwn_ref, vwn_ref, gate_ref):
    cos_t = cos_ref[...]
    sin_t = sin_ref[...]
    seg = seg_ref[...]
    seg1 = seg_ref[0:LANES, 0:LANES]

    h = _rms_rows(x_ref[...], g1_ref[...])
    proj = _mm(h, win_ref[...])

    c_q = _rms_rows(proj[:, 0:256], cqg_ref[...])
    c_kv = _rms_rows(proj[:, 256:384], ckvg_ref[...])

    def store_heads(ref, val):
        for hh in range(val.shape[1] // HEAD_DIM):
            ref[hh] = val[:, hh * HEAD_DIM:(hh + 1) * HEAD_DIM].astype(ref.dtype)

    q_d = _rope(_head_rms(_mm(c_q, wuq_ref[...]), qng_ref[...], seg), cos_t, sin_t) * ATTN_SCALE
    store_heads(qd_ref, q_d)
    q_i = _rope(_mm(c_q, widx_ref[...]), cos_t, sin_t)
    store_heads(qi_ref, q_i)

    kv = _mm(c_kv, wukv_ref[...])
    k_d = _rope(_head_rms(kv, kng_ref[...], seg1), cos_t, sin_t)
    vd_ref[...] = kv[:, 0:HEAD_DIM].astype(vd_ref.dtype)

    idx = proj[:, 384:512]
    mu = _split_mm(idx, seg1)
    d = idx - mu
    var = _split_mm(d * d, seg1)
    k_i = _rope(d * lax.rsqrt(var + EPS) * lng_ref[...] + lnb_ref[...], cos_t, sin_t)
    lane = lax.broadcasted_iota(I32, k_i.shape, 1)
    kk_ref[...] = jnp.where(lane < HEAD_DIM, k_i, k_d).astype(kk_ref.dtype)
    misc_ref[...] = idx * IDX_SCALE

    q_n = _rope(_head_rms(proj[:, 512:1024], nqg_ref[...], seg), cos_t, sin_t) * ATTN_SCALE
    store_heads(qn_ref, q_n)

    kc_ref[...] = proj[:, 1024:1152]
    vc_ref[...] = proj[:, 1152:1280]
    ksl_ref[...] = _rope(_head_rms(proj[:, 1280:1408], nkg_ref[1:2, :], seg1), cos_t, sin_t).astype(ksl_ref.dtype)
    vsl_ref[...] = proj[:, 1408:1536].astype(vsl_ref.dtype)
    kwn_ref[...] = _rope(_head_rms(proj[:, 1536:1664], nkg_ref[2:3, :], seg1), cos_t, sin_t).astype(kwn_ref.dtype)
    vwn_ref[...] = proj[:, 1664:1792].astype(vwn_ref.dtype)
    gate_ref[...] = jax.nn.sigmoid(proj[:, 1792:1920])


def _compress_kernel(xk_ref, xv_ref, pe_ref, w_ref, g_ref, seg_ref, cos_ref, sin_ref, kc_ref, vc_ref):
    n = xk_ref.shape[1]

    def cmp_map(x, which):
        a = _mm(x + pe_ref[2 * which:2 * which + 1, :], w_ref[2 * which])
        b = _mm(x + pe_ref[2 * which + 1:2 * which + 2, :], w_ref[2 * which + 1])
        return a + pltpu.roll(b, n - 1, 0)

    k_c = cmp_map(xk_ref[0], 0)
    k_c = _rope(_head_rms(k_c, g_ref[...], seg_ref[...]), cos_ref[...], sin_ref[...])
    kc_ref[0] = k_c.astype(kc_ref.dtype)
    vc_ref[0] = cmp_map(xv_ref[0], 1).astype(vc_ref.dtype)


def _attend(s, v_t, m_ref, l_ref, acc_ref):
    m_old = m_ref[...]
    m_new = jnp.maximum(m_old, jnp.max(s, axis=0, keepdims=True))
    alpha = jnp.exp(m_old - m_new)
    p = jnp.exp(s - m_new)
    l_ref[...] = alpha * l_ref[...] + jnp.sum(p, axis=0, keepdims=True)
    acc_ref[...] = alpha * acc_ref[...] + _mm(v_t, p)
    m_ref[...] = m_new


def _reset(m_ref, l_ref, acc_ref):
    m_ref[...] = jnp.full(m_ref.shape, NEG_BIG, F32)
    l_ref[...] = jnp.zeros(l_ref.shape, F32)
    acc_ref[...] = jnp.zeros(acc_ref.shape, F32)


def _heads_to_rows(o_t, n_heads, tq):
    stacked = jnp.concatenate([o_t[:, h * tq:(h + 1) * tq] for h in range(n_heads)], axis=0)
    return stacked.T


def _dsa_kernel(qi_ref, qd_ref, wi_ref, kk_ref, vt_ref, o_ref, key_ref, m_ref, l_ref, acc_ref, *, k_sel, idx_bits):
    tq = qi_ref.shape[1]
    tk = vt_ref.shape[3]
    i = pl.program_id(1)
    q0 = i * tq
    n_kb = (q0 + tq + tk - 1) // tk
    t_row = q0 + lax.broadcasted_iota(I32, (1, tq), 1)
    q_i = qi_ref[...].reshape(IDX_HEADS * tq, IDX_DIM)
    q_d = qd_ref[...].reshape(DSA_HEADS * tq, HEAD_DIM)
    w = wi_ref[0]

    def idx_body(j, carry):
        r0 = pl.multiple_of(j * tk, tk)
        kb = kk_ref[0, pl.ds(r0, tk), :]
        s = jnp.maximum(_mm_nt(kb[:, 0:IDX_DIM], q_i), 0.0)
        sc = s[:, 0:tq] * w[0:1, :]
        for hh in range(1, IDX_HEADS):
            sc = sc + s[:, hh * tq:(hh + 1) * tq] * w[hh:hh + 1, :]
        bits = lax.bitcast_convert_type(sc, I32)
        key = bits ^ ((bits >> 31) & jnp.int32(0x7FFFFFFF))
        kpos = r0 + lax.broadcasted_iota(I32, (tk, 1), 0)
        key_ref[pl.ds(r0, tk), :] = jnp.where(kpos <= t_row, key, jnp.int32(INT_MIN))
        return carry

    lax.fori_loop(0, n_kb, idx_body, 0)

    n_chunks = n_kb * (tk // SELECT_CHUNK)

    def count(pred):
        def body(c, cnt):
            r0 = pl.multiple_of(c * SELECT_CHUNK, SELECT_CHUNK)
            rows = r0 + lax.broadcasted_iota(I32, (SELECT_CHUNK, 1), 0)
            f = jnp.where(pred(key_ref[pl.ds(r0, SELECT_CHUNK), :], rows), 1.0, 0.0)
            for r in range(SELECT_CHUNK // SUBLANES):
                cnt = cnt + f[r * SUBLANES:(r + 1) * SUBLANES]
            return cnt
        cnt = lax.fori_loop(0, n_chunks, body, jnp.zeros((SUBLANES, tq), F32))
        return jnp.sum(cnt, axis=0, keepdims=True)

    def bit_body(bi, thr):
        cand = thr + lax.shift_left(jnp.int32(1), 31 - bi)
        tot = count(lambda kb, rows: kb >= cand)
        return jnp.where(tot >= k_sel, cand, thr)

    thr = lax.fori_loop(0, 32, bit_body, jnp.full((1, tq), INT_MIN, I32))

    need = k_sel - count(lambda kb, rows: kb > thr)
    n_eq = count(lambda kb, rows: kb == thr)
    excess = (n_eq > need) & (thr > jnp.int32(INT_MIN))

    @pl.when(jnp.max(jnp.where(excess, 1.0, 0.0)) > 0.5)
    def _():
        def jbit_body(bi, j0):
            cand = j0 + lax.shift_left(jnp.int32(1), idx_bits - 1 - bi)
            tot = count(lambda kb, rows: (kb == thr) & (rows < cand))
            return jnp.where(tot < need, cand, j0)
        j0 = lax.fori_loop(0, idx_bits, jbit_body, jnp.zeros((1, tq), I32))

        def demote(c, carry):
            r0 = pl.multiple_of(c * SELECT_CHUNK, SELECT_CHUNK)
            rows = r0 + lax.broadcasted_iota(I32, (SELECT_CHUNK, 1), 0)
            kb = key_ref[pl.ds(r0, SELECT_CHUNK), :]
            drop = excess & (kb == thr) & (rows > j0)
            key_ref[pl.ds(r0, SELECT_CHUNK), :] = jnp.where(drop, jnp.int32(INT_MIN), kb)
            return carry
        lax.fori_loop(0, n_chunks, demote, 0)

    thr_sel = jnp.maximum(thr, jnp.int32(INT_MIN + 1))

    _reset(m_ref, l_ref, acc_ref)

    def att_body(j, carry):
        r0 = pl.multiple_of(j * tk, tk)
        kb = kk_ref[0, pl.ds(r0, tk), :]
        s = _mm_nt(kb[:, HEAD_DIM:2 * HEAD_DIM], q_d)
        bias = jnp.where(key_ref[pl.ds(r0, tk), :] >= thr_sel, 0.0, -jnp.inf)
        _attend(s + _tile_lanes(bias, DSA_HEADS), vt_ref[0, j], m_ref, l_ref, acc_ref)
        return carry

    lax.fori_loop(0, n_kb, att_body, 0)
    o_t = acc_ref[...] / jnp.maximum(l_ref[...], 1e-30)
    o_ref[...] = _heads_to_rows(o_t, DSA_HEADS, tq).astype(o_ref.dtype)


def _nsa_kernel(qn_ref, kc_ref, vct_ref, ov_ref, ksl_ref, vslt_ref, kwn_ref, vwnt_ref, gate_ref, o_ref,
                sel_ref, m_ref, l_ref, acc_ref, *, n_sel):
    tq = qn_ref.shape[1]
    n_cmp = kc_ref.shape[1]
    n_slc = ov_ref.shape[0]
    tks = vslt_ref.shape[4]
    tkw = vwnt_ref.shape[4]
    r_heads = NSA_GROUP
    nq = r_heads * tq
    i = pl.program_id(1)
    q0 = i * tq
    t_row = q0 + lax.broadcasted_iota(I32, (1, tq), 1)
    t_wide = _tile_lanes(t_row, r_heads)
    gates = gate_ref[0]
    kc = kc_ref[0]
    outs = []

    for g in range(NSA_KV_HEADS):
        lo, hi = g * HEAD_DIM, (g + 1) * HEAD_DIM
        q_g = qn_ref[g * r_heads:(g + 1) * r_heads].reshape(nq, HEAD_DIM)

        def gate_row(branch):
            return jnp.concatenate(
                [gates[(g * r_heads + r) * 3 + branch:(g * r_heads + r) * 3 + branch + 1, :] for r in range(r_heads)],
                axis=1)

        s = _mm_nt(kc[:, lo:hi], q_g)
        cmp_end = lax.broadcasted_iota(I32, (n_cmp, 1), 0) * CMP_STRIDE + (CMP_BLOCK - 1)
        mask = cmp_end <= t_wide
        s = jnp.where(mask, s, -jnp.inf)
        mx = jnp.max(s, axis=0, keepdims=True)
        mx = jnp.where(mx > -jnp.inf, mx, 0.0)
        p = jnp.where(mask, jnp.exp(s - mx), 0.0)
        p = p / jnp.maximum(jnp.sum(p, axis=0, keepdims=True), 1e-30)
        o_g = _mm(vct_ref[0, g], p) * gate_row(0)

        p_sum = p[:, 0:tq]
        for r in range(1, r_heads):
            p_sum = p_sum + p[:, r * tq:(r + 1) * tq]
        blk = _split_mm_left(ov_ref[...], p_sum)
        jrow = lax.broadcasted_iota(I32, (n_slc, 1), 0)
        jrow_f = jrow.astype(F32)
        tb = jnp.right_shift(t_row, SLC_BLOCK.bit_length() - 1)
        bonus = jnp.where(jrow == 0, FORCE_BONUS,
                          jnp.where(jrow == tb, FORCE_BONUS, jnp.where(jrow == tb - 1, FORCE_BONUS, 0.0)))
        val = jnp.where(jrow * SLC_BLOCK <= t_row, blk + bonus, -jnp.inf)
        sel_bias = jnp.full((n_slc, tq), -jnp.inf, F32)
        for _ in range(n_sel):
            top = jnp.max(val, axis=0, keepdims=True)
            first = jnp.min(jnp.where(val == top, jrow_f, float(n_slc)), axis=0, keepdims=True)
            pick = jrow_f == first
            sel_bias = jnp.where(pick, 0.0, sel_bias)
            val = jnp.where(pick, -jnp.inf, val)
        sel_ref[...] = sel_bias

        _reset(m_ref, l_ref, acc_ref)
        n_sb = tks // SLC_BLOCK

        def slc_body(j, carry):
            r0 = pl.multiple_of(j * tks, tks)
            kb = ksl_ref[0, pl.ds(r0, tks), :]
            s2 = _mm_nt(kb[:, lo:hi], q_g)
            rows = sel_ref[pl.ds(pl.multiple_of(j * n_sb, n_sb), n_sb), :]
            picked = jnp.concatenate(
                [jnp.broadcast_to(rows[bb:bb + 1, :], (SLC_BLOCK, tq)) for bb in range(n_sb)], axis=0)
            kpos = r0 + lax.broadcasted_iota(I32, (tks, 1), 0)
            bias = jnp.where(kpos <= t_row, picked, -jnp.inf)
            _attend(s2 + _tile_lanes(bias, r_heads), vslt_ref[0, g, j], m_ref, l_ref, acc_ref)
            return carry

        lax.fori_loop(0, (q0 + tq + tks - 1) // tks, slc_body, 0)
        o_g = o_g + acc_ref[...] / jnp.maximum(l_ref[...], 1e-30) * gate_row(1)

        _reset(m_ref, l_ref, acc_ref)

        def win_body(j, carry):
            r0 = pl.multiple_of(j * tkw, tkw)
            kb = kwn_ref[0, pl.ds(r0, tkw), :]
            s3 = _mm_nt(kb[:, lo:hi], q_g)
            kpos = r0 + lax.broadcasted_iota(I32, (tkw, 1), 0)
            bias = jnp.where(kpos <= t_row, jnp.where(kpos > t_row - WINDOW, 0.0, -jnp.inf), -jnp.inf)
            _attend(s3 + _tile_lanes(bias, r_heads), vwnt_ref[0, g, j], m_ref, l_ref, acc_ref)
            return carry

        j_lo = jnp.maximum(q0 - WINDOW, 0) // tkw
        lax.fori_loop(j_lo, (q0 + tq + tkw - 1) // tkw, win_body, 0)
        o_g = o_g + acc_ref[...] / jnp.maximum(l_ref[...], 1e-30) * gate_row(2)
        outs.append(o_g)

    o_t = jnp.concatenate(outs, axis=1)
    o_ref[...] = _heads_to_rows(o_t, NSA_HEADS, tq).astype(o_ref.dtype)


def _split_mm_left(mat, y):
    hi = y.astype(_MM_DTYPE)
    lo = (y - hi.astype(F32)).astype(_MM_DTYPE)
    return jnp.dot(mat, hi, preferred_element_type=F32) + jnp.dot(mat, lo, preferred_element_type=F32)


def _post_kernel(x_ref, od_ref, on_ref, wo_ref, g2_ref, wq_ref, qg_ref, km_ref, vm_ref, wxo_ref, o_ref):
    half = od_ref.shape[1]
    x1 = x_ref[...] + _mm(od_ref[...], wo_ref[0:half, :]) + _mm(on_ref[...], wo_ref[half:2 * half, :])
    q = _mm(_rms_rows(x1, g2_ref[...]), wq_ref[...])
    km = km_ref[0]
    vm = vm_ref[0]
    heads = []
    for h in range(XATTN_HEADS):
        lo, hi = h * XATTN_HEAD_DIM, (h + 1) * XATTN_HEAD_DIM
        q_h = _rms_rows(q[:, lo:hi], qg_ref[...])
        s = _mm_nt(q_h, km[:, lo:hi]) * (XATTN_HEAD_DIM ** -0.5)
        p = jnp.exp(s - jnp.max(s, axis=-1, keepdims=True))
        heads.append(_mm(p, vm[:, lo:hi]) / jnp.sum(p, axis=-1, keepdims=True))
    o_ref[...] = x1 + _mm(jnp.concatenate(heads, axis=1), wxo_ref[...])


def _mlp_kernel(x_ref, g3_ref, w1_ref, w2_ref, o_ref, h_ref):
    @pl.when(pl.program_id(1) == 0)
    def _():
        x = x_ref[...]
        h_ref[...] = _rms_rows(x, g3_ref[...]).astype(h_ref.dtype)
        o_ref[...] = x

    a = jnp.maximum(jnp.dot(h_ref[...], w1_ref[...], preferred_element_type=F32), 0.0)
    o_ref[...] += _mm(a * a, w2_ref[...])


def _params(n_axes):
    return pltpu.CompilerParams(dimension_semantics=("parallel",) * n_axes, vmem_limit_bytes=VMEM_LIMIT)


def _full(shape):
    return pl.BlockSpec(shape, lambda *_: (0,) * len(shape))


def _rope_tables(pos):
    inv = ROPE_THETA ** (-jnp.arange(HALF_DIM, dtype=F32) / HALF_DIM)
    ang = pos.astype(F32)[:, None] * inv[None, :]
    cos, sin = jnp.cos(ang), jnp.sin(ang)
    reps = LANES // HEAD_DIM
    return (jnp.tile(jnp.concatenate([cos, cos], axis=1), (1, reps)),
            jnp.tile(jnp.concatenate([-sin, sin], axis=1), (1, reps)))


def _tile_gain(g, width):
    return jnp.tile(g.astype(F32), width // g.shape[0])[None, :]


def _layer(x, mem, norm1_g, w_in, dsa_cq_g, dsa_ckv_g, w_dsa_uq, w_dsa_ukv, w_idx_q, idx_k_ln_g, idx_k_ln_b,
           dsa_qn_g, dsa_kn_g, nsa_cmp_pe, w_nsa_cmp, nsa_qn_g, nsa_kn_g, w_out, norm2_g, mem_norm_g,
           w_xq, w_xk, w_xv, xq_norm_g, xk_norm_g, w_xo, norm3_g, w_ff_in, w_ff_out):
    B, L, D = x.shape
    M = mem.shape[1]
    mm = _MM_DTYPE
    T = B * L
    assert L % SLC_KEY_TILE == 0 and L % ROW_TILE == 0 and T % MLP_ROW_TILE == 0
    n_q = L // Q_TILE
    n_slc = L // SLC_BLOCK
    n_cmp_pad = L // CMP_STRIDE
    d_ff = w_ff_in.shape[1]
    xd = XATTN_HEADS * XATTN_HEAD_DIM

    z = lambda n: jnp.zeros((D, n), F32)
    w_in_p = jnp.concatenate([
        w_in[:, 0:448], w_in[:, 448:456], z(56),
        w_in[:, 456:968],
        w_in[:, 968:1736],
        w_in[:, 1736:1760], z(104)], axis=1).astype(mm)
    w_ukv_p = jnp.concatenate([w_dsa_ukv[:, HEAD_DIM:], w_dsa_ukv[:, :HEAD_DIM]], axis=1).astype(mm)
    seg_np = np.kron(np.eye(512 // HEAD_DIM, dtype=np.float32), np.full((HEAD_DIM, HEAD_DIM), 1.0 / HEAD_DIM, np.float32))
    seg = jnp.asarray(seg_np).astype(mm)
    cos_t, sin_t = _rope_tables(jnp.arange(L, dtype=jnp.int32))
    cos_c, sin_c = _rope_tables(jnp.arange(n_cmp_pad, dtype=jnp.int32) * CMP_STRIDE + (CMP_BLOCK - 1))
    pad64 = jnp.zeros((HEAD_DIM,), F32)
    ln_g = jnp.concatenate([idx_k_ln_g, pad64])[None, :]
    ln_b = jnp.concatenate([idx_k_ln_b, pad64])[None, :]
    nkg = jnp.tile(nsa_kn_g, (1, LANES // HEAD_DIM))
    row = lambda v: v.astype(F32)[None, :]

    k_mem, v_mem = pl.pallas_call(
        _memkv_kernel,
        grid=(B,),
        in_specs=[pl.BlockSpec((1, M, D), lambda b: (b, 0, 0)), _full((1, D)), _full((D, xd)), _full((D, xd)),
                  _full((1, XATTN_HEAD_DIM))],
        out_specs=[pl.BlockSpec((1, M, xd), lambda b: (b, 0, 0))] * 2,
        out_shape=[jax.ShapeDtypeStruct((B, M, xd), mm)] * 2,
        compiler_params=_params(1), name="memkv",
    )(mem, row(mem_norm_g), w_xk.astype(mm), w_xv.astype(mm), row(xk_norm_g))

    n_rt = T // ROW_TILE
    rt_per_seq = L // ROW_TILE
    rows = lambda w: pl.BlockSpec((ROW_TILE, w), lambda i: (i, 0))
    heads = pl.BlockSpec((DSA_HEADS, ROW_TILE, HEAD_DIM), lambda i: (0, i, 0))
    table = pl.BlockSpec((ROW_TILE, LANES), lambda i: (i % rt_per_seq, 0))
    sds = jax.ShapeDtypeStruct
    head_shape = sds((DSA_HEADS, T, HEAD_DIM), mm)
    (qd, qi, qn, kk, vd, misc, kc_raw, vc_raw, ksl, vsl, kwn, vwn, gate) = pl.pallas_call(
        _prologue_kernel,
        grid=(n_rt,),
        in_specs=[rows(D), _full((1, D)), _full(w_in_p.shape), _full((1, DSA_Q_RANK)), _full((1, DSA_KV_RANK)),
                  _full(w_dsa_uq.shape), _full(w_ukv_p.shape), _full(w_idx_q.shape),
                  _full((1, LANES)), _full((1, LANES)), _full((1, 512)), _full((1, LANES)), _full((1, 512)),
                  _full((3, LANES)), _full((512, 512)), table, table],
        out_specs=[heads, heads, heads, rows(LANES), rows(HEAD_DIM), rows(LANES), rows(LANES), rows(LANES),
                   rows(LANES), rows(LANES), rows(LANES), rows(LANES), rows(LANES)],
        out_shape=[head_shape, head_shape, head_shape, sds((T, LANES), mm), sds((T, HEAD_DIM), mm),
                   sds((T, LANES), F32), sds((T, LANES), F32), sds((T, LANES), F32),
                   sds((T, LANES), mm), sds((T, LANES), mm), sds((T, LANES), mm), sds((T, LANES), mm),
                   sds((T, LANES), F32)],
        compiler_params=_params(1), name="prologue",
    )(x.reshape(T, D), row(norm1_g), w_in_p, row(dsa_cq_g), row(dsa_ckv_g), w_dsa_uq.astype(mm), w_ukv_p,
      w_idx_q.astype(mm), ln_g, ln_b, _tile_gain(dsa_qn_g, 512), _tile_gain(dsa_kn_g, LANES),
      _tile_gain(nsa_qn_g, 512), nkg, seg, cos_t, sin_t)

    def keys_t(v, tile):
        g = v.shape[1] // HEAD_DIM
        return v.reshape(B, L // tile, tile, g, HEAD_DIM).transpose(0, 3, 1, 4, 2)

    wi_t = misc[:, HEAD_DIM:HEAD_DIM + IDX_HEADS].reshape(B, L, IDX_HEADS).transpose(0, 2, 1)
    gate_t = gate[:, 0:NSA_HEADS * 3].reshape(B, L, NSA_HEADS * 3).transpose(0, 2, 1)
    vd_t = keys_t(vd, DSA_KEY_TILE)[:, 0]
    vsl_t = keys_t(vsl, SLC_KEY_TILE)
    vwn_t = keys_t(vwn, Q_TILE)

    half = CMP_BLOCK // 2
    eye_g = jnp.eye(NSA_KV_HEADS, dtype=F32)

    def cmp_weight(w):
        return jnp.einsum('lde,gh->lgdhe', w, eye_g).reshape(half * NSA_KV_HEADS * HEAD_DIM, NSA_KV_HEADS * HEAD_DIM)

    def cmp_pe(p):
        return jnp.broadcast_to(p[:, None, :], (half, NSA_KV_HEADS, HEAD_DIM)).reshape(1, -1)

    w_cmp = jnp.stack([cmp_weight(w_nsa_cmp[0, :half]), cmp_weight(w_nsa_cmp[0, half:]),
                       cmp_weight(w_nsa_cmp[1, :half]), cmp_weight(w_nsa_cmp[1, half:])]).astype(mm)
    pe_cmp = jnp.concatenate([cmp_pe(nsa_cmp_pe[0, :half]), cmp_pe(nsa_cmp_pe[0, half:]),
                              cmp_pe(nsa_cmp_pe[1, :half]), cmp_pe(nsa_cmp_pe[1, half:])], axis=0)
    cw = half * LANES
    k_c, v_c = pl.pallas_call(
        _compress_kernel,
        grid=(B,),
        in_specs=[pl.BlockSpec((1, n_cmp_pad, cw), lambda b: (b, 0, 0))] * 2
        + [_full((4, cw)), _full((4, cw, LANES)), _full((1, LANES)), _full((LANES, LANES)),
           _full((n_cmp_pad, LANES)), _full((n_cmp_pad, LANES))],
        out_specs=[pl.BlockSpec((1, n_cmp_pad, LANES), lambda b: (b, 0, 0))] * 2,
        out_shape=[sds((B, n_cmp_pad, LANES), mm)] * 2,
        compiler_params=_params(1), name="compress",
    )(kc_raw.reshape(B, n_cmp_pad, cw), vc_raw.reshape(B, n_cmp_pad, cw), pe_cmp, w_cmp, nkg[0:1], seg[:LANES, :LANES],
      cos_c, sin_c)
    vc_t = v_c.reshape(B, n_cmp_pad, NSA_KV_HEADS, HEAD_DIM).transpose(0, 2, 3, 1)

    k_sel = min(DSA_TOPK_MAX, L // 4)
    q_heads = pl.BlockSpec((DSA_HEADS, Q_TILE, HEAD_DIM), lambda b, i: (0, b * n_q + i, 0))
    per_seq = lambda shape: pl.BlockSpec((1,) + shape, lambda b, i: (b,) + (0,) * len(shape))
    out_rows = pl.BlockSpec((Q_TILE, DSA_HEADS * HEAD_DIM), lambda b, i: (b * n_q + i, 0))
    n_lanes = DSA_HEADS * Q_TILE
    o_dsa = pl.pallas_call(
        functools.partial(_dsa_kernel, k_sel=k_sel, idx_bits=int(L - 1).bit_length()),
        grid=(B, n_q),
        in_specs=[q_heads, q_heads, pl.BlockSpec((1, IDX_HEADS, Q_TILE), lambda b, i: (b, 0, i)),
                  per_seq((L, LANES)), per_seq((L // DSA_KEY_TILE, HEAD_DIM, DSA_KEY_TILE))],
        out_specs=out_rows,
        out_shape=sds((T, DSA_HEADS * HEAD_DIM), mm),
        scratch_shapes=[pltpu.VMEM((L, Q_TILE), I32), pltpu.VMEM((1, n_lanes), F32), pltpu.VMEM((1, n_lanes), F32),
                        pltpu.VMEM((HEAD_DIM, n_lanes), F32)],
        compiler_params=_params(2), name="dsa",
    )(qi, qd, wi_t, kk.reshape(B, L, LANES), vd_t)

    ci = np.arange(n_cmp_pad)[None, :] * CMP_STRIDE
    sj = np.arange(n_slc)[:, None] * SLC_BLOCK
    ov_np = ((ci < sj + SLC_BLOCK) & (ci + CMP_BLOCK > sj) & (np.arange(n_cmp_pad)[None, :] < n_cmp_pad - 1))
    ov_t = jnp.asarray(ov_np.astype(np.float32)).astype(mm)
    n_lanes = NSA_GROUP * Q_TILE
    o_nsa = pl.pallas_call(
        functools.partial(_nsa_kernel, n_sel=min(SLC_COUNT, n_slc)),
        grid=(B, n_q),
        in_specs=[q_heads, per_seq((n_cmp_pad, LANES)), per_seq((NSA_KV_HEADS, HEAD_DIM, n_cmp_pad)),
                  pl.BlockSpec((n_slc, n_cmp_pad), lambda b, i: (0, 0)),
                  per_seq((L, LANES)), per_seq((NSA_KV_HEADS, L // SLC_KEY_TILE, HEAD_DIM, SLC_KEY_TILE)),
                  per_seq((L, LANES)), per_seq((NSA_KV_HEADS, L // Q_TILE, HEAD_DIM, Q_TILE)),
                  pl.BlockSpec((1, NSA_HEADS * 3, Q_TILE), lambda b, i: (b, 0, i))],
        out_specs=out_rows,
        out_shape=sds((T, NSA_HEADS * HEAD_DIM), mm),
        scratch_shapes=[pltpu.VMEM((n_slc, Q_TILE), F32), pltpu.VMEM((1, n_lanes), F32), pltpu.VMEM((1, n_lanes), F32),
                        pltpu.VMEM((HEAD_DIM, n_lanes), F32)],
        compiler_params=_params(2), name="nsa",
    )(qn, k_c, vc_t, ov_t, ksl.reshape(B, L, LANES), vsl_t, kwn.reshape(B, L, LANES), vwn_t, gate_t)

    x2 = pl.pallas_call(
        _post_kernel,
        grid=(n_rt,),
        in_specs=[rows(D), rows(512), rows(512), _full((D, D)), _full((1, D)), _full((D, xd)),
                  _full((1, XATTN_HEAD_DIM)),
                  pl.BlockSpec((1, M, xd), lambda i: (i // rt_per_seq, 0, 0)),
                  pl.BlockSpec((1, M, xd), lambda i: (i // rt_per_seq, 0, 0)), _full((xd, D))],
        out_specs=rows(D),
        out_shape=sds((T, D), F32),
        compiler_params=_params(1), name="post",
    )(x.reshape(T, D), o_dsa, o_nsa, w_out.astype(mm), row(norm2_g), w_xq.astype(mm), row(xq_norm_g), k_mem, v_mem,
      w_xo.astype(mm))

    x3 = pl.pallas_call(
        _mlp_kernel,
        grid=(T // MLP_ROW_TILE, d_ff // MLP_FF_TILE),
        in_specs=[pl.BlockSpec((MLP_ROW_TILE, D), lambda i, k: (i, 0)), pl.BlockSpec((1, D), lambda i, k: (0, 0)),
                  pl.BlockSpec((D, MLP_FF_TILE), lambda i, k: (0, k)),
                  pl.BlockSpec((MLP_FF_TILE, D), lambda i, k: (k, 0))],
        out_specs=pl.BlockSpec((MLP_ROW_TILE, D), lambda i, k: (i, 0)),
        out_shape=sds((T, D), F32),
        scratch_shapes=[pltpu.VMEM((MLP_ROW_TILE, D), mm)],
        compiler_params=pltpu.CompilerParams(dimension_semantics=("parallel", "arbitrary"),
                                             vmem_limit_bytes=VMEM_LIMIT),
        name="mlp",
    )(x2, row(norm3_g), w_ff_in.astype(mm), w_ff_out.astype(mm))
    return x3.reshape(B, L, D)


def kernel(x, mem, norm1_g, w_in, dsa_cq_g, dsa_ckv_g, w_dsa_uq, w_dsa_ukv, w_idx_q, idx_k_ln_g, idx_k_ln_b, dsa_qn_g, dsa_kn_g, nsa_cmp_pe, w_nsa_cmp, nsa_qn_g, nsa_kn_g, w_out, norm2_g, mem_norm_g, w_xq, w_xk, w_xv, xq_norm_g, xk_norm_g, w_xo, norm3_g, w_ff_in, w_ff_out):
    params = (norm1_g, w_in, dsa_cq_g, dsa_ckv_g, w_dsa_uq, w_dsa_ukv, w_idx_q, idx_k_ln_g, idx_k_ln_b, dsa_qn_g,
              dsa_kn_g, nsa_cmp_pe, w_nsa_cmp, nsa_qn_g, nsa_kn_g, w_out, norm2_g, mem_norm_g, w_xq, w_xk, w_xv,
              xq_norm_g, xk_norm_g, w_xo, norm3_g, w_ff_in, w_ff_out)
    for layer in range(norm1_g.shape[0]):
        x = _layer(x, mem, *(p[layer] for p in params))
    return x
```

```python
import functools
import math

import jax
import jax.numpy as jnp
import numpy as np
from jax import lax
from jax.experimental import pallas as pl
from jax.experimental.pallas import tpu as pltpu

F32 = jnp.float32
I32 = jnp.int32
_MM_DTYPE = jnp.bfloat16

HEAD_DIM = 64
HALF_DIM = HEAD_DIM // 2
ROPE_THETA = 10000.0
EPS = 1e-6
ATTN_SCALE = HEAD_DIM ** -0.5
LOG2E = math.log2(math.e)

DSA_HEADS = 8
DSA_Q_RANK = 256
DSA_KV_RANK = 128
IDX_HEADS = 8
IDX_DIM = 64
IDX_SCALE = IDX_HEADS ** -0.5 * IDX_DIM ** -0.5
DSA_TOPK_MAX = 256

NSA_HEADS = 8
NSA_KV_HEADS = 2
NSA_GROUP = NSA_HEADS // NSA_KV_HEADS
CMP_BLOCK = 32
CMP_STRIDE = 16
SLC_BLOCK = 64
SLC_COUNT = 16
WINDOW = 512
FORCE_BONUS = 1e4

XATTN_HEADS = 4
XATTN_HEAD_DIM = 128

LANES = 128
SUBLANES = 8
INT_MIN = -(2 ** 31)
KEY_LOWEST_FINITE = INT_MIN + 0x00800000
F32_LOWEST = float(np.finfo(np.float32).min)
NEG_BIG = -1e30
VMEM_LIMIT = 56 * 1024 * 1024

ROW_TILE = 256
MLP_ROW_TILE = 1024
MLP_FF_TILE = 1024
Q_TILE = 128
KEY_TILE = 256
WIN_TILE = 128


def _mm(a, b):
    return jnp.dot(a.astype(_MM_DTYPE), b.astype(_MM_DTYPE), preferred_element_type=F32)


def _mm_nt(a, b):
    return lax.dot_general(a.astype(_MM_DTYPE), b.astype(_MM_DTYPE), (((1,), (1,)), ((), ())),
                           preferred_element_type=F32)


def _split(y):
    hi = y.astype(_MM_DTYPE)
    return hi, (y - hi.astype(F32)).astype(_MM_DTYPE)


def _split_mm(y, mat):
    hi, lo = _split(y)
    return jnp.dot(hi, mat, preferred_element_type=F32) + jnp.dot(lo, mat, preferred_element_type=F32)


def _split_mm_left(mat, y):
    hi, lo = _split(y)
    return jnp.dot(mat, hi, preferred_element_type=F32) + jnp.dot(mat, lo, preferred_element_type=F32)


def _rms_rows(x, g):
    return x * lax.rsqrt(jnp.mean(x * x, axis=-1, keepdims=True) + EPS) * g


def _head_rms(x, g, seg):
    return x * lax.rsqrt(_split_mm(x * x, seg) + EPS) * g


def _tile_lanes(t, n):
    return t if n == 1 else jnp.concatenate([t] * n, axis=1)


def _rope(y, cos_t, sin_t):
    w = y.shape[-1]
    lane = lax.broadcasted_iota(I32, y.shape, 1)
    swapped = jnp.where((lane & HALF_DIM) == 0, pltpu.roll(y, w - HALF_DIM, 1), pltpu.roll(y, HALF_DIM, 1))
    n = w // LANES
    return y * _tile_lanes(cos_t, n) + swapped * _tile_lanes(sin_t, n)


def _store_heads(ref, val):
    for hh in range(val.shape[1] // HEAD_DIM):
        ref[hh] = val[:, hh * HEAD_DIM:(hh + 1) * HEAD_DIM].astype(ref.dtype)


def _tree_sum(parts):
    while len(parts) > 1:
        parts = [parts[a] + parts[a + 1] for a in range(0, len(parts) - 1, 2)] + (
            [parts[-1]] if len(parts) % 2 else [])
    return parts[0]


def _memkv_kernel(mem_ref, g_ref, wk_ref, wv_ref, kg_ref, k_ref, v_ref):
    m = _rms_rows(mem_ref[0], g_ref[...])
    k = _mm(m, wk_ref[...])
    v = _mm(m, wv_ref[...])
    ks = []
    for h in range(XATTN_HEADS):
        ks.append(_rms_rows(k[:, h * XATTN_HEAD_DIM:(h + 1) * XATTN_HEAD_DIM], kg_ref[...]))
    k_ref[0] = jnp.concatenate(ks, axis=1).astype(k_ref.dtype)
    v_ref[0] = v.astype(v_ref.dtype)


def _prologue_kernel(x_ref, g1_ref, win_ref, cqg_ref, ckvg_ref, wuq_ref, wukv_ref, widx_ref,
                     lng_ref, lnb_ref, qng_ref, kng_ref, nqg_ref, nkg_ref, seg_ref, cos_ref, sin_ref,
                     qd_ref, qi_ref, qn_ref, ki_ref, kd_ref, vd_ref, misc_ref, kc_ref, vc_ref,
                     ksl_ref, vsl_ref, kwn_ref, vwn_ref, gate_ref):
    cos_t = cos_ref[...]
    sin_t = sin_ref[...]
    seg = seg_ref[...]
    seg1 = seg_ref[0:LANES, 0:LANES]
    q_scale = ATTN_SCALE * LOG2E

    h = _rms_rows(x_ref[...], g1_ref[...])
    proj = _mm(h, win_ref[...])

    c_q = _rms_rows(proj[:, 0:256], cqg_ref[...])
    c_kv = _rms_rows(proj[:, 256:384], ckvg_ref[...])

    q_d = _rope(_head_rms(_mm(c_q, wuq_ref[...]), qng_ref[...], seg), cos_t, sin_t) * q_scale
    _store_heads(qd_ref, q_d)
    q_i = _rope(_mm(c_q, widx_ref[...]), cos_t, sin_t)
    _store_heads(qi_ref, q_i)

    kv = _mm(c_kv, wukv_ref[...])
    k_d = _rope(_head_rms(kv, kng_ref[...], seg1), cos_t, sin_t)
    kd_ref[...] = k_d[:, 0:HEAD_DIM].astype(kd_ref.dtype)
    vd_ref[...] = kv[:, HEAD_DIM:2 * HEAD_DIM].astype(vd_ref.dtype)

    idx = proj[:, 384:512]
    mu = _split_mm(idx, seg1)
    d = idx - mu
    var = _split_mm(d * d, seg1)
    k_i = _rope(d * lax.rsqrt(var + EPS) * lng_ref[...] + lnb_ref[...], cos_t, sin_t)
    ki_ref[...] = k_i[:, 0:IDX_DIM].astype(ki_ref.dtype)
    misc_ref[...] = idx * IDX_SCALE

    q_n = _rope(_head_rms(proj[:, 512:1024], nqg_ref[...], seg), cos_t, sin_t) * q_scale
    _store_heads(qn_ref, q_n)

    kc_ref[...] = proj[:, 1024:1152]
    vc_ref[...] = proj[:, 1152:1280]
    _store_heads(ksl_ref, _rope(_head_rms(proj[:, 1280:1408], nkg_ref[1:2, :], seg1), cos_t, sin_t))
    vsl_ref[...] = proj[:, 1408:1536].astype(vsl_ref.dtype)
    _store_heads(kwn_ref, _rope(_head_rms(proj[:, 1536:1664], nkg_ref[2:3, :], seg1), cos_t, sin_t))
    vwn_ref[...] = proj[:, 1664:1792].astype(vwn_ref.dtype)
    gate_ref[...] = jax.nn.sigmoid(proj[:, 1792:1920])


def _compress_kernel(xk_ref, xv_ref, pe_ref, w_ref, g_ref, seg_ref, cos_ref, sin_ref, kc_ref, vc_ref):
    n = xk_ref.shape[1]

    def cmp_map(x, which):
        a = _mm(x + pe_ref[2 * which:2 * which + 1, :], w_ref[2 * which])
        b = _mm(x + pe_ref[2 * which + 1:2 * which + 2, :], w_ref[2 * which + 1])
        return a + pltpu.roll(b, n - 1, 0)

    k_c = cmp_map(xk_ref[0], 0)
    k_c = _rope(_head_rms(k_c, g_ref[...], seg_ref[...]), cos_ref[...], sin_ref[...])
    _store_heads(kc_ref.at[0], k_c)
    vc_ref[0] = cmp_map(xv_ref[0], 1).astype(vc_ref.dtype)


def _flash_pipelined(n_super, qk_fn, bias_fn, v_fn, s_buf, p_buf, acc_ref):
    n_lanes = acc_ref.shape[1]

    def softmax_step(slot, jj, half, m, l):
        s = s_buf[slot] + bias_fn(jj, half)
        m_new = jnp.maximum(m, jnp.max(s, axis=0, keepdims=True))
        alpha = jnp.exp2(m - m_new)
        p = jnp.exp2(s - m_new)
        p_buf[slot] = p.astype(p_buf.dtype)
        return m_new, alpha * l + jnp.sum(p, axis=0, keepdims=True), alpha

    def value_step(slot, jj, half, alpha):
        acc_ref[...] = alpha * acc_ref[...] + jnp.dot(v_fn(jj, half), p_buf[slot], preferred_element_type=F32)

    s_buf[0] = qk_fn(0, 0)
    p_buf[1] = jnp.zeros(p_buf.shape[1:], p_buf.dtype)
    acc_ref[...] = jnp.zeros(acc_ref.shape, F32)

    def body(jj, carry):
        m, l, alpha = carry
        s_buf[1] = qk_fn(jj, 1)
        m, l, alpha0 = softmax_step(0, jj, 0, m, l)
        value_step(1, jnp.maximum(jj - 1, 0), 1, alpha)
        s_buf[0] = qk_fn(jnp.minimum(jj + 1, n_super - 1), 0)
        m, l, alpha1 = softmax_step(1, jj, 1, m, l)
        value_step(0, jj, 0, alpha0)
        return m, l, alpha1

    init = (jnp.full((1, n_lanes), NEG_BIG, F32), jnp.zeros((1, n_lanes), F32), jnp.ones((1, n_lanes), F32))
    _, l, alpha = lax.fori_loop(0, n_super, body, init)
    value_step(1, n_super - 1, 1, alpha)
    return l


def _softmax_block(s, bias):
    s = s + bias
    mx = jnp.max(s, axis=0, keepdims=True)
    mx = jnp.where(mx > -jnp.inf, mx, 0.0)
    p = jnp.exp2(s - mx)
    return p, jnp.maximum(jnp.sum(p, axis=0, keepdims=True), 1e-30)


def _heads_to_rows(o_t, n_heads, tq):
    stacked = jnp.concatenate([o_t[:, h * tq:(h + 1) * tq] for h in range(n_heads)], axis=0)
    return stacked.T


def _key_to_float(key):
    bits = key ^ ((key >> 31) & jnp.int32(0x7FFFFFFF))
    return lax.bitcast_convert_type(bits, F32)


def _dsa_kernel(qi_ref, qd_ref, wi_ref, ki_ref, kd_ref, vt_ref, o_ref, sc_ref, s_buf, p_buf, acc_ref, *,
                k_sel, idx_bits):
    tq = qi_ref.shape[1]
    tk = vt_ref.shape[3]
    i = pl.program_id(1)
    q0 = i * tq
    n_super = (q0 + tq + 2 * tk - 1) // (2 * tk)
    n_kb = 2 * n_super
    t_row = q0 + lax.broadcasted_iota(I32, (1, tq), 1)
    w = wi_ref[0]
    q_i = qi_ref[...].reshape(IDX_HEADS * tq, IDX_DIM)
    q_d = qd_ref[...].reshape(DSA_HEADS * tq, HEAD_DIM)

    def block_rows(jj, half):
        return pl.multiple_of((2 * jj + half) * tk, tk)

    def idx_scores(jj, half):
        return _mm_nt(ki_ref[0, pl.ds(block_rows(jj, half), tk), :], q_i)

    def idx_finish(slot, jj, half):
        s = s_buf[slot]
        sc = jnp.maximum(s[:, 0:tq], 0.0) * w[0:1, :]
        for hh in range(1, IDX_HEADS):
            sc = sc + jnp.maximum(s[:, hh * tq:(hh + 1) * tq], 0.0) * w[hh:hh + 1, :]
        r0 = block_rows(jj, half)
        kpos = r0 + lax.broadcasted_iota(I32, (tk, 1), 0)
        sc_ref[pl.ds(r0, tk), :] = jnp.where(kpos <= t_row, sc, -jnp.inf)

    s_buf[0] = idx_scores(0, 0)

    def idx_body(jj, carry):
        s_buf[1] = idx_scores(jj, 1)
        idx_finish(0, jj, 0)
        s_buf[0] = idx_scores(jnp.minimum(jj + 1, n_super - 1), 0)
        idx_finish(1, jj, 1)
        return carry

    lax.fori_loop(0, n_super, idx_body, 0)

    def count(indicator):
        def body(c, cnt):
            r0 = pl.multiple_of(c * tk, tk)
            rows = r0 + lax.broadcasted_iota(I32, (tk, 1), 0)
            f = indicator(sc_ref[pl.ds(r0, tk), :], rows)
            return cnt + _tree_sum([f[r * SUBLANES:(r + 1) * SUBLANES] for r in range(tk // SUBLANES)])
        cnt = lax.fori_loop(0, n_kb, body, jnp.zeros((SUBLANES, tq), F32))
        return jnp.sum(cnt, axis=0, keepdims=True)

    def bit_body(bi, thr):
        cand = thr + lax.shift_left(jnp.int32(1), 31 - bi)
        cand_f = _key_to_float(cand)
        tot = count(lambda v, rows: jnp.where(v >= cand_f, 1.0, 0.0))
        return jnp.where(tot >= k_sel, cand, thr)

    thr = lax.fori_loop(0, 32, bit_body, jnp.full((1, tq), INT_MIN, I32))
    few = thr < jnp.int32(KEY_LOWEST_FINITE)
    thr_f = jnp.where(few, F32_LOWEST, _key_to_float(jnp.maximum(thr, jnp.int32(KEY_LOWEST_FINITE))))

    need = k_sel - count(lambda v, rows: jnp.where(v > thr_f, 1.0, 0.0))
    n_eq = count(lambda v, rows: jnp.where(v == thr_f, 1.0, 0.0))
    excess = jnp.where(few, 0.0, jnp.where(n_eq > need, 1.0, 0.0))

    @pl.when(jnp.max(excess) > 0.5)
    def _():
        def jbit_body(bi, j0):
            cand = j0 + lax.shift_left(jnp.int32(1), idx_bits - 1 - bi)
            tot = count(lambda v, rows: jnp.where(rows < cand, jnp.where(v == thr_f, 1.0, 0.0), 0.0))
            return jnp.where(tot < need, cand, j0)
        j0 = lax.fori_loop(0, idx_bits, jbit_body, jnp.zeros((1, tq), I32))

        def demote(c, carry):
            r0 = pl.multiple_of(c * tk, tk)
            rows = r0 + lax.broadcasted_iota(I32, (tk, 1), 0)
            v = sc_ref[pl.ds(r0, tk), :]
            demoted = jnp.where(rows > j0, jnp.where(v == thr_f, -jnp.inf, v), v)
            sc_ref[pl.ds(r0, tk), :] = jnp.where(excess > 0.5, demoted, v)
            return carry
        lax.fori_loop(0, n_kb, demote, 0)

    def qk_fn(jj, half):
        return _mm_nt(kd_ref[0, pl.ds(block_rows(jj, half), tk), :], q_d)

    def bias_fn(jj, half):
        sel = jnp.where(sc_ref[pl.ds(block_rows(jj, half), tk), :] >= thr_f, 0.0, -jnp.inf)
        return _tile_lanes(sel, DSA_HEADS)

    l = _flash_pipelined(n_super, qk_fn, bias_fn, lambda jj, half: vt_ref[0, 2 * jj + half], s_buf, p_buf, acc_ref)
    o_t = acc_ref[...] / jnp.maximum(l, 1e-30)
    o_ref[...] = _heads_to_rows(o_t, DSA_HEADS, tq).astype(o_ref.dtype)


def _nsa_kernel(qn_ref, kc_ref, vct_ref, ov_ref, ksl_ref, vslt_ref, kwn_ref, vwnt_ref, gate_ref, o_ref,
                sel_ref, s_buf, p_buf, acc_ref, out_ref, *, n_sel):
    tq = qn_ref.shape[1]
    n_cmp = kc_ref.shape[2]
    n_slc = ov_ref.shape[0]
    tk = vslt_ref.shape[4]
    tkw = vwnt_ref.shape[4]
    blk_per_tile = tk // SLC_BLOCK
    r_heads = NSA_GROUP
    nq = r_heads * tq
    win_keys = WINDOW + tq
    i = pl.program_id(1)
    q0 = i * tq
    t_row = q0 + lax.broadcasted_iota(I32, (1, tq), 1)
    gates = gate_ref[0]
    q_groups = [qn_ref[g * r_heads:(g + 1) * r_heads].reshape(nq, HEAD_DIM) for g in range(NSA_KV_HEADS)]

    def gate_wide(g, branch):
        return jnp.concatenate(
            [gates[(g * r_heads + r) * 3 + branch:(g * r_heads + r) * 3 + branch + 1, :] for r in range(r_heads)],
            axis=1)

    for g in range(NSA_KV_HEADS):
        lanes = slice(g * nq, (g + 1) * nq)

        cmp_end = lax.broadcasted_iota(I32, (n_cmp, 1), 0) * CMP_STRIDE + (CMP_BLOCK - 1)
        bias = _tile_lanes(jnp.where(cmp_end <= t_row, 0.0, -jnp.inf), r_heads)
        p, l = _softmax_block(_mm_nt(kc_ref[0, g], q_groups[g]), bias)
        p = p / l
        o_g = _mm(vct_ref[0, g], p) * gate_wide(g, 0)

        p_sum = p[:, 0:tq]
        for r in range(1, r_heads):
            p_sum = p_sum + p[:, r * tq:(r + 1) * tq]
        blk = _split_mm_left(ov_ref[...], p_sum)
        jrow = lax.broadcasted_iota(I32, (n_slc, 1), 0)
        jrow_f = jrow.astype(F32)
        tb = jnp.right_shift(t_row, SLC_BLOCK.bit_length() - 1)
        bonus = jnp.where(jrow == 0, FORCE_BONUS,
                          jnp.where(jrow == tb, FORCE_BONUS, jnp.where(jrow == tb - 1, FORCE_BONUS, 0.0)))
        val = jnp.where(jrow * SLC_BLOCK <= t_row, blk + bonus, -jnp.inf)
        sel_bias = jnp.full((n_slc, tq), -jnp.inf, F32)
        for _ in range(n_sel):
            top = jnp.max(val, axis=0, keepdims=True)
            first = jnp.min(jnp.where(val == top, jrow_f, float(n_slc)), axis=0, keepdims=True)
            pick = jrow_f == first
            sel_bias = jnp.where(pick, 0.0, sel_bias)
            val = jnp.where(pick, -jnp.inf, val)
        sel_ref[g] = sel_bias

        start = pl.multiple_of(jnp.maximum(q0 - WINDOW, 0), tkw)
        kpos = start + lax.broadcasted_iota(I32, (win_keys, 1), 0)
        bias = jnp.where(kpos <= t_row, jnp.where(kpos > t_row - WINDOW, 0.0, -jnp.inf), -jnp.inf)
        p, l = _softmax_block(_mm_nt(kwn_ref[g, pl.ds(start, win_keys), :], q_groups[g]),
                              _tile_lanes(bias, r_heads))
        o_w = None
        for k in range(win_keys // tkw):
            part = _mm(vwnt_ref[0, g, start // tkw + k], p[k * tkw:(k + 1) * tkw, :])
            o_w = part if o_w is None else o_w + part
        out_ref[:, lanes] = o_g + o_w / l * gate_wide(g, 2)

    n_super = (q0 + tq + 2 * tk - 1) // (2 * tk)
    for g in range(NSA_KV_HEADS):
        lanes = slice(g * nq, (g + 1) * nq)

        def qk_fn(jj, half, g=g):
            r0 = pl.multiple_of((2 * jj + half) * tk, tk)
            return _mm_nt(ksl_ref[g, pl.ds(r0, tk), :], q_groups[g])

        def bias_fn(jj, half, g=g):
            r0 = pl.multiple_of((2 * jj + half) * tk, tk)
            picked = sel_ref[g, pl.ds(pl.multiple_of(jj * 2 * blk_per_tile, 2 * blk_per_tile), 2 * blk_per_tile), :]
            blocks = [jnp.broadcast_to(picked[half * blk_per_tile + bb:half * blk_per_tile + bb + 1, :], (SLC_BLOCK, tq))
                      for bb in range(blk_per_tile)]
            kpos = r0 + lax.broadcasted_iota(I32, (tk, 1), 0)
            return _tile_lanes(jnp.where(kpos <= t_row, jnp.concatenate(blocks, axis=0), -jnp.inf), r_heads)

        l = _flash_pipelined(n_super, qk_fn, bias_fn, lambda jj, half, g=g: vslt_ref[0, g, 2 * jj + half],
                             s_buf, p_buf, acc_ref)
        out_ref[:, lanes] = out_ref[:, lanes] + acc_ref[...] / jnp.maximum(l, 1e-30) * gate_wide(g, 1)

    o_ref[...] = _heads_to_rows(out_ref[...], NSA_HEADS, tq).astype(o_ref.dtype)


def _post_kernel(x_ref, od_ref, on_ref, wo_ref, g2_ref, wq_ref, qg_ref, km_ref, vm_ref, wxo_ref, o_ref):
    half = od_ref.shape[1]
    x1 = x_ref[...] + _mm(od_ref[...], wo_ref[0:half, :]) + _mm(on_ref[...], wo_ref[half:2 * half, :])
    q = _mm(_rms_rows(x1, g2_ref[...]), wq_ref[...])
    km = km_ref[0]
    vm = vm_ref[0]
    heads = []
    for h in range(XATTN_HEADS):
        lo, hi = h * XATTN_HEAD_DIM, (h + 1) * XATTN_HEAD_DIM
        q_h = _rms_rows(q[:, lo:hi], qg_ref[...])
        s = _mm_nt(q_h, km[:, lo:hi]) * (XATTN_HEAD_DIM ** -0.5)
        p = jnp.exp(s - jnp.max(s, axis=-1, keepdims=True))
        heads.append(_mm(p, vm[:, lo:hi]) / jnp.sum(p, axis=-1, keepdims=True))
    o_ref[...] = x1 + _mm(jnp.concatenate(heads, axis=1), wxo_ref[...])


def _mlp_kernel(x_ref, g3_ref, w1_ref, w2_ref, o_ref, h_ref):
    @pl.when(pl.program_id(1) == 0)
    def _():
        x = x_ref[...]
        h_ref[...] = _rms_rows(x, g3_ref[...]).astype(h_ref.dtype)
        o_ref[...] = x

    a = jnp.maximum(jnp.dot(h_ref[...], w1_ref[...], preferred_element_type=F32), 0.0)
    o_ref[...] += _mm(a * a, w2_ref[...])


def _params(n_axes):
    return pltpu.CompilerParams(dimension_semantics=("parallel",) * n_axes, vmem_limit_bytes=VMEM_LIMIT)


def _full(shape):
    return pl.BlockSpec(shape, lambda *_: (0,) * len(shape))


def _rope_tables(pos):
    inv = ROPE_THETA ** (-jnp.arange(HALF_DIM, dtype=F32) / HALF_DIM)
    ang = pos.astype(F32)[:, None] * inv[None, :]
    cos, sin = jnp.cos(ang), jnp.sin(ang)
    reps = LANES // HEAD_DIM
    return (jnp.tile(jnp.concatenate([cos, cos], axis=1), (1, reps)),
            jnp.tile(jnp.concatenate([-sin, sin], axis=1), (1, reps)))


def _tile_gain(g, width):
    return jnp.tile(g.astype(F32), width // g.shape[0])[None, :]


def _layer(x, mem, norm1_g, w_in, dsa_cq_g, dsa_ckv_g, w_dsa_uq, w_dsa_ukv, w_idx_q, idx_k_ln_g, idx_k_ln_b,
           dsa_qn_g, dsa_kn_g, nsa_cmp_pe, w_nsa_cmp, nsa_qn_g, nsa_kn_g, w_out, norm2_g, mem_norm_g,
           w_xq, w_xk, w_xv, xq_norm_g, xk_norm_g, w_xo, norm3_g, w_ff_in, w_ff_out):
    B, L, D = x.shape
    M = mem.shape[1]
    mm = _MM_DTYPE
    T = B * L
    assert L % (2 * KEY_TILE) == 0 and L % ROW_TILE == 0 and T % MLP_ROW_TILE == 0 and L >= WINDOW + Q_TILE
    n_q = L // Q_TILE
    n_slc = L // SLC_BLOCK
    n_cmp_pad = L // CMP_STRIDE
    d_ff = w_ff_in.shape[1]
    xd = XATTN_HEADS * XATTN_HEAD_DIM
    G = NSA_KV_HEADS

    z = lambda n: jnp.zeros((D, n), F32)
    w_in_p = jnp.concatenate([
        w_in[:, 0:448], w_in[:, 448:456], z(56),
        w_in[:, 456:968],
        w_in[:, 968:1736],
        w_in[:, 1736:1760], z(104)], axis=1).astype(mm)
    seg_np = np.kron(np.eye(512 // HEAD_DIM, dtype=np.float32), np.full((HEAD_DIM, HEAD_DIM), 1.0 / HEAD_DIM, np.float32))
    seg = jnp.asarray(seg_np).astype(mm)
    cos_t, sin_t = _rope_tables(jnp.arange(L, dtype=jnp.int32))
    cos_c, sin_c = _rope_tables(jnp.arange(n_cmp_pad, dtype=jnp.int32) * CMP_STRIDE + (CMP_BLOCK - 1))
    pad64 = jnp.zeros((HEAD_DIM,), F32)
    ln_g = jnp.concatenate([idx_k_ln_g, pad64])[None, :]
    ln_b = jnp.concatenate([idx_k_ln_b, pad64])[None, :]
    nkg = jnp.tile(nsa_kn_g, (1, LANES // HEAD_DIM))
    row = lambda v: v.astype(F32)[None, :]
    sds = jax.ShapeDtypeStruct

    k_mem, v_mem = pl.pallas_call(
        _memkv_kernel,
        grid=(B,),
        in_specs=[pl.BlockSpec((1, M, D), lambda b: (b, 0, 0)), _full((1, D)), _full((D, xd)), _full((D, xd)),
                  _full((1, XATTN_HEAD_DIM))],
        out_specs=[pl.BlockSpec((1, M, xd), lambda b: (b, 0, 0))] * 2,
        out_shape=[sds((B, M, xd), mm)] * 2,
        compiler_params=_params(1), name="memkv",
    )(mem, row(mem_norm_g), w_xk.astype(mm), w_xv.astype(mm), row(xk_norm_g))

    n_rt = T // ROW_TILE
    rt_per_seq = L // ROW_TILE
    rows = lambda w: pl.BlockSpec((ROW_TILE, w), lambda i: (i, 0))
    heads = lambda n: pl.BlockSpec((n, ROW_TILE, HEAD_DIM), lambda i: (0, i, 0))
    table = pl.BlockSpec((ROW_TILE, LANES), lambda i: (i % rt_per_seq, 0))
    head_shape = lambda n: sds((n, T, HEAD_DIM), mm)
    (qd, qi, qn, ki, kd, vd, misc, kc_raw, vc_raw, ksl, vsl, kwn, vwn, gate) = pl.pallas_call(
        _prologue_kernel,
        grid=(n_rt,),
        in_specs=[rows(D), _full((1, D)), _full(w_in_p.shape), _full((1, DSA_Q_RANK)), _full((1, DSA_KV_RANK)),
                  _full(w_dsa_uq.shape), _full(w_dsa_ukv.shape), _full(w_idx_q.shape),
                  _full((1, LANES)), _full((1, LANES)), _full((1, 512)), _full((1, LANES)), _full((1, 512)),
                  _full((3, LANES)), _full((512, 512)), table, table],
        out_specs=[heads(DSA_HEADS), heads(IDX_HEADS), heads(NSA_HEADS), rows(HEAD_DIM), rows(HEAD_DIM),
                   rows(HEAD_DIM), rows(LANES), rows(LANES), rows(LANES),
                   heads(G), rows(LANES), heads(G), rows(LANES), rows(LANES)],
        out_shape=[head_shape(DSA_HEADS), head_shape(IDX_HEADS), head_shape(NSA_HEADS), sds((T, HEAD_DIM), mm),
                   sds((T, HEAD_DIM), mm), sds((T, HEAD_DIM), mm), sds((T, LANES), F32), sds((T, LANES), F32),
                   sds((T, LANES), F32), head_shape(G), sds((T, LANES), mm), head_shape(G), sds((T, LANES), mm),
                   sds((T, LANES), F32)],
        compiler_params=_params(1), name="prologue",
    )(x.reshape(T, D), row(norm1_g), w_in_p, row(dsa_cq_g), row(dsa_ckv_g), w_dsa_uq.astype(mm),
      w_dsa_ukv.astype(mm), w_idx_q.astype(mm), ln_g, ln_b, _tile_gain(dsa_qn_g, 512),
      _tile_gain(dsa_kn_g, LANES), _tile_gain(nsa_qn_g, 512), nkg, seg, cos_t, sin_t)

    def keys_t(v, tile):
        g = v.shape[1] // HEAD_DIM
        return v.reshape(B, L // tile, tile, g, HEAD_DIM).transpose(0, 3, 1, 4, 2)

    wi_t = misc[:, IDX_DIM:IDX_DIM + IDX_HEADS].reshape(B, L, IDX_HEADS).transpose(0, 2, 1)
    gate_t = gate[:, 0:NSA_HEADS * 3].reshape(B, L, NSA_HEADS * 3).transpose(0, 2, 1)
    vd_t = keys_t(vd, KEY_TILE)[:, 0]
    vsl_t = keys_t(vsl, KEY_TILE)
    vwn_t = keys_t(vwn, WIN_TILE)

    half = CMP_BLOCK // 2
    eye_g = jnp.eye(G, dtype=F32)

    def cmp_weight(w):
        return jnp.einsum('lde,gh->lgdhe', w, eye_g).reshape(half * G * HEAD_DIM, G * HEAD_DIM)

    def cmp_pe(p):
        return jnp.broadcast_to(p[:, None, :], (half, G, HEAD_DIM)).reshape(1, -1)

    w_cmp = jnp.stack([cmp_weight(w_nsa_cmp[0, :half]), cmp_weight(w_nsa_cmp[0, half:]),
                       cmp_weight(w_nsa_cmp[1, :half]), cmp_weight(w_nsa_cmp[1, half:])]).astype(mm)
    pe_cmp = jnp.concatenate([cmp_pe(nsa_cmp_pe[0, :half]), cmp_pe(nsa_cmp_pe[0, half:]),
                              cmp_pe(nsa_cmp_pe[1, :half]), cmp_pe(nsa_cmp_pe[1, half:])], axis=0)
    cw = half * LANES
    k_c, v_c = pl.pallas_call(
        _compress_kernel,
        grid=(B,),
        in_specs=[pl.BlockSpec((1, n_cmp_pad, cw), lambda b: (b, 0, 0))] * 2
        + [_full((4, cw)), _full((4, cw, LANES)), _full((1, LANES)), _full((LANES, LANES)),
           _full((n_cmp_pad, LANES)), _full((n_cmp_pad, LANES))],
        out_specs=[pl.BlockSpec((1, G, n_cmp_pad, HEAD_DIM), lambda b: (b, 0, 0, 0)),
                   pl.BlockSpec((1, n_cmp_pad, LANES), lambda b: (b, 0, 0))],
        out_shape=[sds((B, G, n_cmp_pad, HEAD_DIM), mm), sds((B, n_cmp_pad, LANES), mm)],
        compiler_params=_params(1), name="compress",
    )(kc_raw.reshape(B, n_cmp_pad, cw), vc_raw.reshape(B, n_cmp_pad, cw), pe_cmp, w_cmp, nkg[0:1], seg[:LANES, :LANES],
      cos_c, sin_c)
    vc_t = v_c.reshape(B, n_cmp_pad, G, HEAD_DIM).transpose(0, 2, 3, 1)

    k_sel = min(DSA_TOPK_MAX, L // 4)
    q_heads = pl.BlockSpec((DSA_HEADS, Q_TILE, HEAD_DIM), lambda b, i: (0, b * n_q + i, 0))
    per_seq = lambda shape: pl.BlockSpec((1,) + shape, lambda b, i: (b,) + (0,) * len(shape))
    seq_heads = pl.BlockSpec((G, L, HEAD_DIM), lambda b, i: (0, b, 0))
    out_rows = pl.BlockSpec((Q_TILE, DSA_HEADS * HEAD_DIM), lambda b, i: (b * n_q + i, 0))
    o_dsa = pl.pallas_call(
        functools.partial(_dsa_kernel, k_sel=k_sel, idx_bits=int(L - 1).bit_length()),
        grid=(B, n_q),
        in_specs=[q_heads, q_heads, pl.BlockSpec((1, IDX_HEADS, Q_TILE), lambda b, i: (b, 0, i)),
                  per_seq((L, IDX_DIM)), per_seq((L, HEAD_DIM)), per_seq((L // KEY_TILE, HEAD_DIM, KEY_TILE))],
        out_specs=out_rows,
        out_shape=sds((T, DSA_HEADS * HEAD_DIM), mm),
        scratch_shapes=[pltpu.VMEM((L, Q_TILE), F32), pltpu.VMEM((2, KEY_TILE, DSA_HEADS * Q_TILE), F32),
                        pltpu.VMEM((2, KEY_TILE, DSA_HEADS * Q_TILE), mm),
                        pltpu.VMEM((HEAD_DIM, DSA_HEADS * Q_TILE), F32)],
        compiler_params=_params(2), name="dsa",
    )(qi, qd, wi_t, ki.reshape(B, L, IDX_DIM), kd.reshape(B, L, HEAD_DIM), vd_t)

    ci = np.arange(n_cmp_pad)[None, :] * CMP_STRIDE
    sj = np.arange(n_slc)[:, None] * SLC_BLOCK
    ov_np = ((ci < sj + SLC_BLOCK) & (ci + CMP_BLOCK > sj) & (np.arange(n_cmp_pad)[None, :] < n_cmp_pad - 1))
    ov_t = jnp.asarray(ov_np.astype(np.float32)).astype(mm)
    o_nsa = pl.pallas_call(
        functools.partial(_nsa_kernel, n_sel=min(SLC_COUNT, n_slc)),
        grid=(B, n_q),
        in_specs=[q_heads, per_seq((G, n_cmp_pad, HEAD_DIM)), per_seq((G, HEAD_DIM, n_cmp_pad)),
                  pl.BlockSpec((n_slc, n_cmp_pad), lambda b, i: (0, 0)),
                  seq_heads, per_seq((G, L // KEY_TILE, HEAD_DIM, KEY_TILE)),
                  seq_heads, per_seq((G, L // WIN_TILE, HEAD_DIM, WIN_TILE)),
                  pl.BlockSpec((1, NSA_HEADS * 3, Q_TILE), lambda b, i: (b, 0, i))],
        out_specs=out_rows,
        out_shape=sds((T, NSA_HEADS * HEAD_DIM), mm),
        scratch_shapes=[pltpu.VMEM((G, n_slc, Q_TILE), F32), pltpu.VMEM((2, KEY_TILE, NSA_GROUP * Q_TILE), F32),
                        pltpu.VMEM((2, KEY_TILE, NSA_GROUP * Q_TILE), mm),
                        pltpu.VMEM((HEAD_DIM, NSA_GROUP * Q_TILE), F32),
                        pltpu.VMEM((HEAD_DIM, NSA_HEADS * Q_TILE), F32)],
        compiler_params=_params(2), name="nsa",
    )(qn, k_c, vc_t, ov_t, ksl, vsl_t, kwn, vwn_t, gate_t)

    x2 = pl.pallas_call(
        _post_kernel,
        grid=(n_rt,),
        in_specs=[rows(D), rows(512), rows(512), _full((D, D)), _full((1, D)), _full((D, xd)),
                  _full((1, XATTN_HEAD_DIM)),
                  pl.BlockSpec((1, M, xd), lambda i: (i // rt_per_seq, 0, 0)),
                  pl.BlockSpec((1, M, xd), lambda i: (i // rt_per_seq, 0, 0)), _full((xd, D))],
        out_specs=rows(D),
        out_shape=sds((T, D), F32),
        compiler_params=_params(1), name="post",
    )(x.reshape(T, D), o_dsa, o_nsa, w_out.astype(mm), row(norm2_g), w_xq.astype(mm), row(xq_norm_g), k_mem, v_mem,
      w_xo.astype(mm))

    x3 = pl.pallas_call(
        _mlp_kernel,
        grid=(T // MLP_ROW_TILE, d_ff // MLP_FF_TILE),
        in_specs=[pl.BlockSpec((MLP_ROW_TILE, D), lambda i, k: (i, 0)), pl.BlockSpec((1, D), lambda i, k: (0, 0)),
                  pl.BlockSpec((D, MLP_FF_TILE), lambda i, k: (0, k)),
                  pl.BlockSpec((MLP_FF_TILE, D), lambda i, k: (k, 0))],
        out_specs=pl.BlockSpec((MLP_ROW_TILE, D), lambda i, k: (i, 0)),
        out_shape=sds((T, D), F32),
        scratch_shapes=[pltpu.VMEM((MLP_ROW_TILE, D), mm)],
        compiler_params=pltpu.CompilerParams(dimension_semantics=("parallel", "arbitrary"),
                                             vmem_limit_bytes=VMEM_LIMIT),
        name="mlp",
    )(x2, row(norm3_g), w_ff_in.astype(mm), w_ff_out.astype(mm))
    return x3.reshape(B, L, D)


def kernel(x, mem, norm1_g, w_in, dsa_cq_g, dsa_ckv_g, w_dsa_uq, w_dsa_ukv, w_idx_q, idx_k_ln_g, idx_k_ln_b, dsa_qn_g, dsa_kn_g, nsa_cmp_pe, w_nsa_cmp, nsa_qn_g, nsa_kn_g, w_out, norm2_g, mem_norm_g, w_xq, w_xk, w_xv, xq_norm_g, xk_norm_g, w_xo, norm3_g, w_ff_in, w_ff_out):
    params = (norm1_g, w_in, dsa_cq_g, dsa_ckv_g, w_dsa_uq, w_dsa_ukv, w_idx_q, idx_k_ln_g, idx_k_ln_b, dsa_qn_g,
              dsa_kn_g, nsa_cmp_pe, w_nsa_cmp, nsa_qn_g, nsa_kn_g, w_out, norm2_g, mem_norm_g, w_xq, w_xk, w_xv,
              xq_norm_g, xk_norm_g, w_xo, norm3_g, w_ff_in, w_ff_out)
    for layer in range(norm1_g.shape[0]):
        x = _layer(x, mem, *(p[layer] for p in params))
    return x
```

```python
import functools
import math

import jax
import jax.numpy as jnp
import numpy as np
from jax import lax
from jax.experimental import pallas as pl
from jax.experimental.pallas import tpu as pltpu

F32 = jnp.float32
I32 = jnp.int32
_MM_DTYPE = jnp.bfloat16

HEAD_DIM = 64
HALF_DIM = HEAD_DIM // 2
ROPE_THETA = 10000.0
EPS = 1e-6
ATTN_SCALE = HEAD_DIM ** -0.5
LOG2E = math.log2(math.e)

DSA_HEADS = 8
DSA_Q_RANK = 256
DSA_KV_RANK = 128
IDX_HEADS = 8
IDX_DIM = 64
IDX_SCALE = IDX_HEADS ** -0.5 * IDX_DIM ** -0.5
DSA_TOPK_MAX = 256

NSA_HEADS = 8
NSA_KV_HEADS = 2
NSA_GROUP = NSA_HEADS // NSA_KV_HEADS
CMP_BLOCK = 32
CMP_STRIDE = 16
SLC_BLOCK = 64
SLC_COUNT = 16
WINDOW = 512
FORCE_BONUS = 1e4

XATTN_HEADS = 4
XATTN_HEAD_DIM = 128

LANES = 128
SUBLANES = 8
INT_MIN = -(2 ** 31)
KEY_LOWEST_FINITE = INT_MIN + 0x00800000
F32_LOWEST = float(np.finfo(np.float32).min)
NEG_BIG = -1e30
VMEM_LIMIT = 56 * 1024 * 1024

ROW_TILE = 256
MLP_ROW_TILE = 1024
MLP_FF_TILE = 1024
Q_TILE = 128
KEY_TILE = 256
WIN_TILE = 128


def _mm(a, b):
    return jnp.dot(a.astype(_MM_DTYPE), b.astype(_MM_DTYPE), preferred_element_type=F32)


def _mm_nt(a, b):
    return lax.dot_general(a.astype(_MM_DTYPE), b.astype(_MM_DTYPE), (((1,), (1,)), ((), ())),
                           preferred_element_type=F32)


def _split(y):
    hi = y.astype(_MM_DTYPE)
    return hi, (y - hi.astype(F32)).astype(_MM_DTYPE)


def _split_mm(y, mat):
    hi, lo = _split(y)
    return jnp.dot(hi, mat, preferred_element_type=F32) + jnp.dot(lo, mat, preferred_element_type=F32)


def _split_mm_left(mat, y):
    hi, lo = _split(y)
    return jnp.dot(mat, hi, preferred_element_type=F32) + jnp.dot(mat, lo, preferred_element_type=F32)


def _rms_rows(x, g):
    return x * lax.rsqrt(jnp.mean(x * x, axis=-1, keepdims=True) + EPS) * g


def _head_rms(x, g, seg):
    return x * lax.rsqrt(_split_mm(x * x, seg) + EPS) * g


def _tile_lanes(t, n):
    return t if n == 1 else jnp.concatenate([t] * n, axis=1)


def _rope(y, cos_t, sin_t):
    w = y.shape[-1]
    lane = lax.broadcasted_iota(I32, y.shape, 1)
    swapped = jnp.where((lane & HALF_DIM) == 0, pltpu.roll(y, w - HALF_DIM, 1), pltpu.roll(y, HALF_DIM, 1))
    n = w // LANES
    return y * _tile_lanes(cos_t, n) + swapped * _tile_lanes(sin_t, n)


def _store_heads(ref, val):
    for hh in range(val.shape[1] // HEAD_DIM):
        ref[hh] = val[:, hh * HEAD_DIM:(hh + 1) * HEAD_DIM].astype(ref.dtype)


def _store_value_tiles(ref, v_t):
    tile = ref.shape[3]
    for g in range(ref.shape[0]):
        for k in range(ref.shape[1]):
            ref[g, k] = v_t[g * HEAD_DIM:(g + 1) * HEAD_DIM, k * tile:(k + 1) * tile].astype(ref.dtype)


def _tree_sum(parts):
    while len(parts) > 1:
        parts = [parts[a] + parts[a + 1] for a in range(0, len(parts) - 1, 2)] + (
            [parts[-1]] if len(parts) % 2 else [])
    return parts[0]


def _memkv_kernel(mem_ref, g_ref, wk_ref, wv_ref, kg_ref, k_ref, v_ref):
    m = _rms_rows(mem_ref[0], g_ref[...])
    k = _mm(m, wk_ref[...])
    v = _mm(m, wv_ref[...])
    ks = []
    for h in range(XATTN_HEADS):
        ks.append(_rms_rows(k[:, h * XATTN_HEAD_DIM:(h + 1) * XATTN_HEAD_DIM], kg_ref[...]))
    k_ref[0] = jnp.concatenate(ks, axis=1).astype(k_ref.dtype)
    v_ref[0] = v.astype(v_ref.dtype)


def _prologue_kernel(x_ref, g1_ref, win_ref, cqg_ref, ckvg_ref, wuq_ref, wukv_ref, widx_ref,
                     lng_ref, lnb_ref, qng_ref, kng_ref, nqg_ref, nkg_ref, seg_ref, cos_ref, sin_ref,
                     qd_ref, qi_ref, qn_ref, ki_ref, kd_ref, vdt_ref, wi_ref, kc_ref, vc_ref,
                     ksl_ref, vslt_ref, kwn_ref, vwnt_ref, gate_ref):
    cos_t = cos_ref[...]
    sin_t = sin_ref[...]
    seg = seg_ref[...]
    seg1 = seg_ref[0:LANES, 0:LANES]
    q_scale = ATTN_SCALE * LOG2E

    h = _rms_rows(x_ref[...], g1_ref[...])
    proj = _mm(h, win_ref[...])

    c_q = _rms_rows(proj[:, 0:256], cqg_ref[...])
    c_kv = _rms_rows(proj[:, 256:384], ckvg_ref[...])

    q_d = _rope(_head_rms(_mm(c_q, wuq_ref[...]), qng_ref[...], seg), cos_t, sin_t) * q_scale
    _store_heads(qd_ref, q_d)
    q_i = _rope(_mm(c_q, widx_ref[...]), cos_t, sin_t)
    _store_heads(qi_ref, q_i)

    kv = _mm(c_kv, wukv_ref[...])
    k_d = _rope(_head_rms(kv, kng_ref[...], seg1), cos_t, sin_t)
    kd_ref[...] = k_d[:, 0:HEAD_DIM].astype(kd_ref.dtype)
    vdt_ref[0] = kv.T[HEAD_DIM:2 * HEAD_DIM, :].astype(vdt_ref.dtype)

    idx = proj[:, 384:512]
    mu = _split_mm(idx, seg1)
    d = idx - mu
    var = _split_mm(d * d, seg1)
    k_i = _rope(d * lax.rsqrt(var + EPS) * lng_ref[...] + lnb_ref[...], cos_t, sin_t)
    ki_ref[...] = k_i[:, 0:IDX_DIM].astype(ki_ref.dtype)
    wi_ref[...] = (idx * IDX_SCALE).T[IDX_DIM:IDX_DIM + IDX_HEADS, :]

    q_n = _rope(_head_rms(proj[:, 512:1024], nqg_ref[...], seg), cos_t, sin_t) * q_scale
    _store_heads(qn_ref, q_n)

    kc_ref[...] = proj[:, 1024:1152]
    vc_ref[...] = proj[:, 1152:1280]
    _store_heads(ksl_ref, _rope(_head_rms(proj[:, 1280:1408], nkg_ref[1:2, :], seg1), cos_t, sin_t))
    _store_value_tiles(vslt_ref, proj[:, 1408:1536].T)
    _store_heads(kwn_ref, _rope(_head_rms(proj[:, 1536:1664], nkg_ref[2:3, :], seg1), cos_t, sin_t))
    _store_value_tiles(vwnt_ref, proj[:, 1664:1792].T)
    gate_ref[...] = jax.nn.sigmoid(proj[:, 1792:1920]).T[0:gate_ref.shape[0], :]


def _compress_kernel(xk_ref, xv_ref, pe_ref, w_ref, g_ref, seg_ref, cos_ref, sin_ref, kc_ref, vc_ref):
    n = xk_ref.shape[1]

    def cmp_map(x, which):
        a = _mm(x + pe_ref[2 * which:2 * which + 1, :], w_ref[2 * which])
        b = _mm(x + pe_ref[2 * which + 1:2 * which + 2, :], w_ref[2 * which + 1])
        return a + pltpu.roll(b, n - 1, 0)

    k_c = cmp_map(xk_ref[0], 0)
    k_c = _rope(_head_rms(k_c, g_ref[...], seg_ref[...]), cos_ref[...], sin_ref[...])
    _store_heads(kc_ref.at[0], k_c)
    vc_ref[0] = cmp_map(xv_ref[0], 1).astype(vc_ref.dtype)


def _flash_pipelined(n_super, scores_fn, v_fn, s_buf, p_buf, acc_ref):
    n_lanes = acc_ref.shape[1]

    def softmax_step(slot, m, l):
        m_new = jnp.maximum(m, jnp.max(s_buf[slot], axis=0, keepdims=True))
        alpha = jnp.exp2(m - m_new)
        p = jnp.exp2(s_buf[slot] - m_new)
        p_buf[slot] = p.astype(p_buf.dtype)
        return m_new, alpha * l + jnp.sum(p, axis=0, keepdims=True), alpha

    def value_step(slot, jj, half, alpha):
        acc_ref[...] = alpha * acc_ref[...] + jnp.dot(v_fn(jj, half), p_buf[slot], preferred_element_type=F32)

    s_buf[0] = scores_fn(0, 0)
    p_buf[1] = jnp.zeros(p_buf.shape[1:], p_buf.dtype)
    acc_ref[...] = jnp.zeros(acc_ref.shape, F32)

    def body(jj, carry):
        m, l, alpha = carry
        s_buf[1] = scores_fn(jj, 1)
        m, l, alpha0 = softmax_step(0, m, l)
        value_step(1, jnp.maximum(jj - 1, 0), 1, alpha)
        s_buf[0] = scores_fn(jnp.minimum(jj + 1, n_super - 1), 0)
        m, l, alpha1 = softmax_step(1, m, l)
        value_step(0, jj, 0, alpha0)
        return m, l, alpha1

    init = (jnp.full((1, n_lanes), NEG_BIG, F32), jnp.zeros((1, n_lanes), F32), jnp.ones((1, n_lanes), F32))
    _, l, alpha = lax.fori_loop(0, n_super, body, init)
    value_step(1, n_super - 1, 1, alpha)
    return l


def _softmax_block(s, bias):
    s = s + bias
    mx = jnp.max(s, axis=0, keepdims=True)
    mx = jnp.where(mx > -jnp.inf, mx, 0.0)
    p = jnp.exp2(s - mx)
    return p, jnp.maximum(jnp.sum(p, axis=0, keepdims=True), 1e-30)


def _heads_to_rows(o_t, n_heads, tq):
    stacked = jnp.concatenate([o_t[:, h * tq:(h + 1) * tq] for h in range(n_heads)], axis=0)
    return stacked.T


def _key_to_float(key):
    bits = key ^ ((key >> 31) & jnp.int32(0x7FFFFFFF))
    return lax.bitcast_convert_type(bits, F32)


def _dsa_kernel(qi_ref, qd_ref, wi_ref, ki_ref, kd_ref, vt_ref, o_ref, sc_ref, s_buf, p_buf, acc_ref, *,
                k_sel, idx_bits):
    tq = qi_ref.shape[1]
    tk = vt_ref.shape[2]
    i = pl.program_id(1)
    q0 = i * tq
    n_super = (q0 + tq + 2 * tk - 1) // (2 * tk)
    n_kb = 2 * n_super
    t_row = q0 + lax.broadcasted_iota(I32, (1, tq), 1)
    w = wi_ref[...]
    q_i = qi_ref[...].reshape(IDX_HEADS * tq, IDX_DIM)
    q_d = qd_ref[...].reshape(DSA_HEADS * tq, HEAD_DIM)

    def block_rows(jj, half):
        return pl.multiple_of((2 * jj + half) * tk, tk)

    def idx_scores(jj, half):
        return _mm_nt(ki_ref[0, pl.ds(block_rows(jj, half), tk), :], q_i)

    def idx_finish(slot, jj, half):
        s = s_buf[slot]
        sc = jnp.maximum(s[:, 0:tq], 0.0) * w[0:1, :]
        for hh in range(1, IDX_HEADS):
            sc = sc + jnp.maximum(s[:, hh * tq:(hh + 1) * tq], 0.0) * w[hh:hh + 1, :]
        r0 = block_rows(jj, half)
        kpos = r0 + lax.broadcasted_iota(I32, (tk, 1), 0)
        sc_ref[pl.ds(r0, tk), :] = jnp.where(kpos <= t_row, sc, -jnp.inf)

    s_buf[0] = idx_scores(0, 0)

    def idx_body(jj, carry):
        s_buf[1] = idx_scores(jj, 1)
        idx_finish(0, jj, 0)
        s_buf[0] = idx_scores(jnp.minimum(jj + 1, n_super - 1), 0)
        idx_finish(1, jj, 1)
        return carry

    lax.fori_loop(0, n_super, idx_body, 0)

    def count(indicator):
        def body(c, cnt):
            for half in range(2):
                r0 = block_rows(c, half)
                rows = r0 + lax.broadcasted_iota(I32, (tk, 1), 0)
                f = indicator(sc_ref[pl.ds(r0, tk), :], rows)
                cnt = cnt + _tree_sum([f[r * SUBLANES:(r + 1) * SUBLANES] for r in range(tk // SUBLANES)])
            return cnt
        cnt = lax.fori_loop(0, n_super, body, jnp.zeros((SUBLANES, tq), F32))
        return jnp.sum(cnt, axis=0, keepdims=True)

    def bit_cond(state):
        return jnp.logical_and(state[0] < 32, state[1] == 0)

    def bit_body(state):
        bi, _, thr, n_ge, settled = state
        cand = jnp.where(settled > 0.5, thr, thr + lax.shift_left(jnp.int32(1), 31 - bi))
        cand_f = _key_to_float(cand)
        tot = count(lambda v, rows: jnp.where(v >= cand_f, 1.0, 0.0))
        keep = tot >= k_sel
        thr = jnp.where(keep, cand, thr)
        n_ge = jnp.where(keep, tot, n_ge)
        settled = jnp.where(tot == k_sel, 1.0, settled)
        return bi + 1, (jnp.min(settled) > 0.5).astype(I32), thr, n_ge, settled

    _, _, thr, n_ge, _ = lax.while_loop(
        bit_cond, bit_body,
        (jnp.int32(0), jnp.int32(0), jnp.full((1, tq), INT_MIN, I32), jnp.zeros((1, tq), F32),
         jnp.zeros((1, tq), F32)))
    few = thr < jnp.int32(KEY_LOWEST_FINITE)
    thr_f = jnp.where(few, F32_LOWEST, _key_to_float(jnp.maximum(thr, jnp.int32(KEY_LOWEST_FINITE))))

    excess = jnp.where(few, 0.0, jnp.where(n_ge > k_sel, 1.0, 0.0))

    @pl.when(jnp.max(excess) > 0.5)
    def _():
        need = k_sel - count(lambda v, rows: jnp.where(v > thr_f, 1.0, 0.0))

        def jbit_body(bi, j0):
            cand = j0 + lax.shift_left(jnp.int32(1), idx_bits - 1 - bi)
            tot = count(lambda v, rows: jnp.where(rows < cand, jnp.where(v == thr_f, 1.0, 0.0), 0.0))
            return jnp.where(tot < need, cand, j0)
        j0 = lax.fori_loop(0, idx_bits, jbit_body, jnp.zeros((1, tq), I32))

        def demote(c, carry):
            r0 = pl.multiple_of(c * tk, tk)
            rows = r0 + lax.broadcasted_iota(I32, (tk, 1), 0)
            v = sc_ref[pl.ds(r0, tk), :]
            demoted = jnp.where(rows > j0, jnp.where(v == thr_f, -jnp.inf, v), v)
            sc_ref[pl.ds(r0, tk), :] = jnp.where(excess > 0.5, demoted, v)
            return carry
        lax.fori_loop(0, n_kb, demote, 0)

    def scores_fn(jj, half):
        r0 = block_rows(jj, half)
        sel = jnp.where(sc_ref[pl.ds(r0, tk), :] >= thr_f, 0.0, -jnp.inf)
        return _mm_nt(kd_ref[0, pl.ds(r0, tk), :], q_d) + _tile_lanes(sel, DSA_HEADS)

    l = _flash_pipelined(n_super, scores_fn, lambda jj, half: vt_ref[2 * jj + half], s_buf, p_buf, acc_ref)
    o_t = acc_ref[...] / jnp.maximum(l, 1e-30)
    o_ref[...] = _heads_to_rows(o_t, DSA_HEADS, tq).astype(o_ref.dtype)


def _nsa_kernel(qn_ref, kc_ref, vct_ref, ov_ref, ksl_ref, vslt_ref, kwn_ref, vwnt_ref, gate_ref, o_ref,
                sel_ref, s_buf, p_buf, acc_ref, out_ref, *, n_sel):
    tq = qn_ref.shape[1]
    n_cmp = kc_ref.shape[2]
    n_slc = ov_ref.shape[0]
    tk = vslt_ref.shape[3]
    tkw = vwnt_ref.shape[3]
    blk_per_tile = tk // SLC_BLOCK
    r_heads = NSA_GROUP
    nq = r_heads * tq
    win_keys = WINDOW + tq
    i = pl.program_id(1)
    q0 = i * tq
    t_row = q0 + lax.broadcasted_iota(I32, (1, tq), 1)
    gates = gate_ref[...]
    q_groups = [qn_ref[g * r_heads:(g + 1) * r_heads].reshape(nq, HEAD_DIM) for g in range(NSA_KV_HEADS)]

    def gate_wide(g, branch):
        return jnp.concatenate(
            [gates[(g * r_heads + r) * 3 + branch:(g * r_heads + r) * 3 + branch + 1, :] for r in range(r_heads)],
            axis=1)

    for g in range(NSA_KV_HEADS):
        lanes = slice(g * nq, (g + 1) * nq)

        cmp_end = lax.broadcasted_iota(I32, (n_cmp, 1), 0) * CMP_STRIDE + (CMP_BLOCK - 1)
        bias = _tile_lanes(jnp.where(cmp_end <= t_row, 0.0, -jnp.inf), r_heads)
        p, l = _softmax_block(_mm_nt(kc_ref[0, g], q_groups[g]), bias)
        p = p / l
        o_g = _mm(vct_ref[0, g], p) * gate_wide(g, 0)

        p_sum = p[:, 0:tq]
        for r in range(1, r_heads):
            p_sum = p_sum + p[:, r * tq:(r + 1) * tq]
        blk = _split_mm_left(ov_ref[...], p_sum)
        jrow = lax.broadcasted_iota(I32, (n_slc, 1), 0)
        jrow_f = jrow.astype(F32)
        tb = jnp.right_shift(t_row, SLC_BLOCK.bit_length() - 1)
        bonus = jnp.where(jrow == 0, FORCE_BONUS,
                          jnp.where(jrow == tb, FORCE_BONUS, jnp.where(jrow == tb - 1, FORCE_BONUS, 0.0)))
        val = jnp.where(jrow * SLC_BLOCK <= t_row, blk + bonus, -jnp.inf)
        sel_bias = jnp.full((n_slc, tq), -jnp.inf, F32)
        for _ in range(n_sel):
            top = jnp.max(val, axis=0, keepdims=True)
            first = jnp.min(jnp.where(val == top, jrow_f, float(n_slc)), axis=0, keepdims=True)
            pick = jrow_f == first
            sel_bias = jnp.where(pick, 0.0, sel_bias)
            val = jnp.where(pick, -jnp.inf, val)
        sel_ref[g] = sel_bias

        start = pl.multiple_of(jnp.maximum(q0 - WINDOW, 0), tkw)
        kpos = start + lax.broadcasted_iota(I32, (win_keys, 1), 0)
        bias = jnp.where(kpos <= t_row, jnp.where(kpos > t_row - WINDOW, 0.0, -jnp.inf), -jnp.inf)
        p, l = _softmax_block(_mm_nt(kwn_ref[g, pl.ds(start, win_keys), :], q_groups[g]),
                              _tile_lanes(bias, r_heads))
        o_w = None
        for k in range(win_keys // tkw):
            part = _mm(vwnt_ref[g, start // tkw + k], p[k * tkw:(k + 1) * tkw, :])
            o_w = part if o_w is None else o_w + part
        out_ref[:, lanes] = o_g + o_w / l * gate_wide(g, 2)

    n_super = (q0 + tq + 2 * tk - 1) // (2 * tk)
    for g in range(NSA_KV_HEADS):
        lanes = slice(g * nq, (g + 1) * nq)

        def scores_fn(jj, half, g=g):
            r0 = pl.multiple_of((2 * jj + half) * tk, tk)
            picked = sel_ref[g, pl.ds(pl.multiple_of(jj * 2 * blk_per_tile, 2 * blk_per_tile), 2 * blk_per_tile), :]
            blocks = [jnp.broadcast_to(picked[half * blk_per_tile + bb:half * blk_per_tile + bb + 1, :], (SLC_BLOCK, tq))
                      for bb in range(blk_per_tile)]
            kpos = r0 + lax.broadcasted_iota(I32, (tk, 1), 0)
            bias = jnp.where(kpos <= t_row, jnp.concatenate(blocks, axis=0), -jnp.inf)
            return _mm_nt(ksl_ref[g, pl.ds(r0, tk), :], q_groups[g]) + _tile_lanes(bias, r_heads)

        l = _flash_pipelined(n_super, scores_fn, lambda jj, half, g=g: vslt_ref[g, 2 * jj + half],
                             s_buf, p_buf, acc_ref)
        out_ref[:, lanes] = out_ref[:, lanes] + acc_ref[...] / jnp.maximum(l, 1e-30) * gate_wide(g, 1)

    o_ref[...] = _heads_to_rows(out_ref[...], NSA_HEADS, tq).astype(o_ref.dtype)


def _post_kernel(x_ref, od_ref, on_ref, wo_ref, g2_ref, wq_ref, qg_ref, km_ref, vm_ref, wxo_ref, o_ref):
    half = od_ref.shape[1]
    x1 = x_ref[...] + _mm(od_ref[...], wo_ref[0:half, :]) + _mm(on_ref[...], wo_ref[half:2 * half, :])
    q = _mm(_rms_rows(x1, g2_ref[...]), wq_ref[...])
    km = km_ref[0]
    vm = vm_ref[0]
    heads = []
    for h in range(XATTN_HEADS):
        lo, hi = h * XATTN_HEAD_DIM, (h + 1) * XATTN_HEAD_DIM
        q_h = _rms_rows(q[:, lo:hi], qg_ref[...])
        s = _mm_nt(q_h, km[:, lo:hi]) * (XATTN_HEAD_DIM ** -0.5)
        p = jnp.exp(s - jnp.max(s, axis=-1, keepdims=True))
        heads.append(_mm(p, vm[:, lo:hi]) / jnp.sum(p, axis=-1, keepdims=True))
    o_ref[...] = x1 + _mm(jnp.concatenate(heads, axis=1), wxo_ref[...])


def _mlp_kernel(x_ref, g3_ref, w1_ref, w2_ref, o_ref, h_ref):
    @pl.when(pl.program_id(1) == 0)
    def _():
        x = x_ref[...]
        h_ref[...] = _rms_rows(x, g3_ref[...]).astype(h_ref.dtype)
        o_ref[...] = x

    a = jnp.maximum(jnp.dot(h_ref[...], w1_ref[...], preferred_element_type=F32), 0.0)
    o_ref[...] += _mm(a * a, w2_ref[...])


def _params(n_axes):
    return pltpu.CompilerParams(dimension_semantics=("parallel",) * n_axes, vmem_limit_bytes=VMEM_LIMIT)


def _full(shape):
    return pl.BlockSpec(shape, lambda *_: (0,) * len(shape))


def _rope_tables(pos):
    inv = ROPE_THETA ** (-jnp.arange(HALF_DIM, dtype=F32) / HALF_DIM)
    ang = pos.astype(F32)[:, None] * inv[None, :]
    cos, sin = jnp.cos(ang), jnp.sin(ang)
    reps = LANES // HEAD_DIM
    return (jnp.tile(jnp.concatenate([cos, cos], axis=1), (1, reps)),
            jnp.tile(jnp.concatenate([-sin, sin], axis=1), (1, reps)))


def _tile_gain(g, width):
    return jnp.tile(g.astype(F32), width // g.shape[0])[None, :]


def _layer(x, mem, norm1_g, w_in, dsa_cq_g, dsa_ckv_g, w_dsa_uq, w_dsa_ukv, w_idx_q, idx_k_ln_g, idx_k_ln_b,
           dsa_qn_g, dsa_kn_g, nsa_cmp_pe, w_nsa_cmp, nsa_qn_g, nsa_kn_g, w_out, norm2_g, mem_norm_g,
           w_xq, w_xk, w_xv, xq_norm_g, xk_norm_g, w_xo, norm3_g, w_ff_in, w_ff_out):
    B, L, D = x.shape
    M = mem.shape[1]
    mm = _MM_DTYPE
    T = B * L
    assert L % (2 * KEY_TILE) == 0 and ROW_TILE == KEY_TILE and T % MLP_ROW_TILE == 0 and L >= WINDOW + Q_TILE
    n_q = L // Q_TILE
    n_slc = L // SLC_BLOCK
    n_cmp_pad = L // CMP_STRIDE
    d_ff = w_ff_in.shape[1]
    xd = XATTN_HEADS * XATTN_HEAD_DIM
    G = NSA_KV_HEADS

    z = lambda n: jnp.zeros((D, n), F32)
    w_in_p = jnp.concatenate([
        w_in[:, 0:448], w_in[:, 448:456], z(56),
        w_in[:, 456:968],
        w_in[:, 968:1736],
        w_in[:, 1736:1760], z(104)], axis=1).astype(mm)
    seg_np = np.kron(np.eye(512 // HEAD_DIM, dtype=np.float32), np.full((HEAD_DIM, HEAD_DIM), 1.0 / HEAD_DIM, np.float32))
    seg = jnp.asarray(seg_np).astype(mm)
    cos_t, sin_t = _rope_tables(jnp.arange(L, dtype=jnp.int32))
    cos_c, sin_c = _rope_tables(jnp.arange(n_cmp_pad, dtype=jnp.int32) * CMP_STRIDE + (CMP_BLOCK - 1))
    pad64 = jnp.zeros((HEAD_DIM,), F32)
    ln_g = jnp.concatenate([idx_k_ln_g, pad64])[None, :]
    ln_b = jnp.concatenate([idx_k_ln_b, pad64])[None, :]
    nkg = jnp.tile(nsa_kn_g, (1, LANES // HEAD_DIM))
    row = lambda v: v.astype(F32)[None, :]
    sds = jax.ShapeDtypeStruct

    k_mem, v_mem = pl.pallas_call(
        _memkv_kernel,
        grid=(B,),
        in_specs=[pl.BlockSpec((1, M, D), lambda b: (b, 0, 0)), _full((1, D)), _full((D, xd)), _full((D, xd)),
                  _full((1, XATTN_HEAD_DIM))],
        out_specs=[pl.BlockSpec((1, M, xd), lambda b: (b, 0, 0))] * 2,
        out_shape=[sds((B, M, xd), mm)] * 2,
        compiler_params=_params(1), name="memkv",
    )(mem, row(mem_norm_g), w_xk.astype(mm), w_xv.astype(mm), row(xk_norm_g))

    n_rt = T // ROW_TILE
    rt_per_seq = L // ROW_TILE
    rows = lambda w: pl.BlockSpec((ROW_TILE, w), lambda i: (i, 0))
    heads = lambda n: pl.BlockSpec((n, ROW_TILE, HEAD_DIM), lambda i: (0, i, 0))
    table = pl.BlockSpec((ROW_TILE, LANES), lambda i: (i % rt_per_seq, 0))
    head_shape = lambda n: sds((n, T, HEAD_DIM), mm)
    win_per_row_tile = ROW_TILE // WIN_TILE
    (qd, qi, qn, ki, kd, vd_t, wi_t, kc_raw, vc_raw, ksl, vsl_t, kwn, vwn_t, gate_t) = pl.pallas_call(
        _prologue_kernel,
        grid=(n_rt,),
        in_specs=[rows(D), _full((1, D)), _full(w_in_p.shape), _full((1, DSA_Q_RANK)), _full((1, DSA_KV_RANK)),
                  _full(w_dsa_uq.shape), _full(w_dsa_ukv.shape), _full(w_idx_q.shape),
                  _full((1, LANES)), _full((1, LANES)), _full((1, 512)), _full((1, LANES)), _full((1, 512)),
                  _full((3, LANES)), _full((512, 512)), table, table],
        out_specs=[heads(DSA_HEADS), heads(IDX_HEADS), heads(NSA_HEADS), rows(HEAD_DIM), rows(HEAD_DIM),
                   pl.BlockSpec((1, HEAD_DIM, KEY_TILE), lambda i: (i, 0, 0)),
                   pl.BlockSpec((IDX_HEADS, ROW_TILE), lambda i: (0, i)), rows(LANES), rows(LANES),
                   heads(G), pl.BlockSpec((G, 1, HEAD_DIM, KEY_TILE), lambda i: (0, i, 0, 0)),
                   heads(G), pl.BlockSpec((G, win_per_row_tile, HEAD_DIM, WIN_TILE), lambda i: (0, i, 0, 0)),
                   pl.BlockSpec((NSA_HEADS * 3, ROW_TILE), lambda i: (0, i))],
        out_shape=[head_shape(DSA_HEADS), head_shape(IDX_HEADS), head_shape(NSA_HEADS), sds((T, HEAD_DIM), mm),
                   sds((T, HEAD_DIM), mm), sds((n_rt, HEAD_DIM, KEY_TILE), mm), sds((IDX_HEADS, T), F32),
                   sds((T, LANES), F32), sds((T, LANES), F32),
                   head_shape(G), sds((G, n_rt, HEAD_DIM, KEY_TILE), mm),
                   head_shape(G), sds((G, T // WIN_TILE, HEAD_DIM, WIN_TILE), mm),
                   sds((NSA_HEADS * 3, T), F32)],
        compiler_params=_params(1), name="prologue",
    )(x.reshape(T, D), row(norm1_g), w_in_p, row(dsa_cq_g), row(dsa_ckv_g), w_dsa_uq.astype(mm),
      w_dsa_ukv.astype(mm), w_idx_q.astype(mm), ln_g, ln_b, _tile_gain(dsa_qn_g, 512),
      _tile_gain(dsa_kn_g, LANES), _tile_gain(nsa_qn_g, 512), nkg, seg, cos_t, sin_t)

    half = CMP_BLOCK // 2
    eye_g = jnp.eye(G, dtype=F32)

    def cmp_weight(w):
        return jnp.einsum('lde,gh->lgdhe', w, eye_g).reshape(half * G * HEAD_DIM, G * HEAD_DIM)

    def cmp_pe(p):
        return jnp.broadcast_to(p[:, None, :], (half, G, HEAD_DIM)).reshape(1, -1)

    w_cmp = jnp.stack([cmp_weight(w_nsa_cmp[0, :half]), cmp_weight(w_nsa_cmp[0, half:]),
                       cmp_weight(w_nsa_cmp[1, :half]), cmp_weight(w_nsa_cmp[1, half:])]).astype(mm)
    pe_cmp = jnp.concatenate([cmp_pe(nsa_cmp_pe[0, :half]), cmp_pe(nsa_cmp_pe[0, half:]),
                              cmp_pe(nsa_cmp_pe[1, :half]), cmp_pe(nsa_cmp_pe[1, half:])], axis=0)
    cw = half * LANES
    k_c, v_c = pl.pallas_call(
        _compress_kernel,
        grid=(B,),
        in_specs=[pl.BlockSpec((1, n_cmp_pad, cw), lambda b: (b, 0, 0))] * 2
        + [_full((4, cw)), _full((4, cw, LANES)), _full((1, LANES)), _full((LANES, LANES)),
           _full((n_cmp_pad, LANES)), _full((n_cmp_pad, LANES))],
        out_specs=[pl.BlockSpec((1, G, n_cmp_pad, HEAD_DIM), lambda b: (b, 0, 0, 0)),
                   pl.BlockSpec((1, n_cmp_pad, LANES), lambda b: (b, 0, 0))],
        out_shape=[sds((B, G, n_cmp_pad, HEAD_DIM), mm), sds((B, n_cmp_pad, LANES), mm)],
        compiler_params=_params(1), name="compress",
    )(kc_raw.reshape(B, n_cmp_pad, cw), vc_raw.reshape(B, n_cmp_pad, cw), pe_cmp, w_cmp, nkg[0:1], seg[:LANES, :LANES],
      cos_c, sin_c)
    vc_t = v_c.reshape(B, n_cmp_pad, G, HEAD_DIM).transpose(0, 2, 3, 1)

    k_sel = min(DSA_TOPK_MAX, L // 4)
    q_heads = pl.BlockSpec((DSA_HEADS, Q_TILE, HEAD_DIM), lambda b, i: (0, b * n_q + i, 0))
    per_seq = lambda shape: pl.BlockSpec((1,) + shape, lambda b, i: (b,) + (0,) * len(shape))
    seq_heads = pl.BlockSpec((G, L, HEAD_DIM), lambda b, i: (0, b, 0))
    out_rows = pl.BlockSpec((Q_TILE, DSA_HEADS * HEAD_DIM), lambda b, i: (b * n_q + i, 0))
    o_dsa = pl.pallas_call(
        functools.partial(_dsa_kernel, k_sel=k_sel, idx_bits=int(L - 1).bit_length()),
        grid=(B, n_q),
        in_specs=[q_heads, q_heads, pl.BlockSpec((IDX_HEADS, Q_TILE), lambda b, i: (0, b * n_q + i)),
                  per_seq((L, IDX_DIM)), per_seq((L, HEAD_DIM)),
                  pl.BlockSpec((L // KEY_TILE, HEAD_DIM, KEY_TILE), lambda b, i: (b, 0, 0))],
        out_specs=out_rows,
        out_shape=sds((T, DSA_HEADS * HEAD_DIM), mm),
        scratch_shapes=[pltpu.VMEM((L, Q_TILE), F32), pltpu.VMEM((2, KEY_TILE, DSA_HEADS * Q_TILE), F32),
                        pltpu.VMEM((2, KEY_TILE, DSA_HEADS * Q_TILE), mm),
                        pltpu.VMEM((HEAD_DIM, DSA_HEADS * Q_TILE), F32)],
        compiler_params=_params(2), name="dsa",
    )(qi, qd, wi_t, ki.reshape(B, L, IDX_DIM), kd.reshape(B, L, HEAD_DIM), vd_t)

    ci = np.arange(n_cmp_pad)[None, :] * CMP_STRIDE
    sj = np.arange(n_slc)[:, None] * SLC_BLOCK
    ov_np = ((ci < sj + SLC_BLOCK) & (ci + CMP_BLOCK > sj) & (np.arange(n_cmp_pad)[None, :] < n_cmp_pad - 1))
    ov_t = jnp.asarray(ov_np.astype(np.float32)).astype(mm)
    o_nsa = pl.pallas_call(
        functools.partial(_nsa_kernel, n_sel=min(SLC_COUNT, n_slc)),
        grid=(B, n_q),
        in_specs=[q_heads, per_seq((G, n_cmp_pad, HEAD_DIM)), per_seq((G, HEAD_DIM, n_cmp_pad)),
                  pl.BlockSpec((n_slc, n_cmp_pad), lambda b, i: (0, 0)),
                  seq_heads, pl.BlockSpec((G, L // KEY_TILE, HEAD_DIM, KEY_TILE), lambda b, i: (0, b, 0, 0)),
                  seq_heads, pl.BlockSpec((G, L // WIN_TILE, HEAD_DIM, WIN_TILE), lambda b, i: (0, b, 0, 0)),
                  pl.BlockSpec((NSA_HEADS * 3, Q_TILE), lambda b, i: (0, b * n_q + i))],
        out_specs=out_rows,
        out_shape=sds((T, NSA_HEADS * HEAD_DIM), mm),
        scratch_shapes=[pltpu.VMEM((G, n_slc, Q_TILE), F32), pltpu.VMEM((2, KEY_TILE, NSA_GROUP * Q_TILE), F32),
                        pltpu.VMEM((2, KEY_TILE, NSA_GROUP * Q_TILE), mm),
                        pltpu.VMEM((HEAD_DIM, NSA_GROUP * Q_TILE), F32),
                        pltpu.VMEM((HEAD_DIM, NSA_HEADS * Q_TILE), F32)],
        compiler_params=_params(2), name="nsa",
    )(qn, k_c, vc_t, ov_t, ksl, vsl_t, kwn, vwn_t, gate_t)

    x2 = pl.pallas_call(
        _post_kernel,
        grid=(n_rt,),
        in_specs=[rows(D), rows(512), rows(512), _full((D, D)), _full((1, D)), _full((D, xd)),
                  _full((1, XATTN_HEAD_DIM)),
                  pl.BlockSpec((1, M, xd), lambda i: (i // rt_per_seq, 0, 0)),
                  pl.BlockSpec((1, M, xd), lambda i: (i // rt_per_seq, 0, 0)), _full((xd, D))],
        out_specs=rows(D),
        out_shape=sds((T, D), F32),
        compiler_params=_params(1), name="post",
    )(x.reshape(T, D), o_dsa, o_nsa, w_out.astype(mm), row(norm2_g), w_xq.astype(mm), row(xq_norm_g), k_mem, v_mem,
      w_xo.astype(mm))

    x3 = pl.pallas_call(
        _mlp_kernel,
        grid=(T // MLP_ROW_TILE, d_ff // MLP_FF_TILE),
        in_specs=[pl.BlockSpec((MLP_ROW_TILE, D), lambda i, k: (i, 0)), pl.BlockSpec((1, D), lambda i, k: (0, 0)),
                  pl.BlockSpec((D, MLP_FF_TILE), lambda i, k: (0, k)),
                  pl.BlockSpec((MLP_FF_TILE, D), lambda i, k: (k, 0))],
        out_specs=pl.BlockSpec((MLP_ROW_TILE, D), lambda i, k: (i, 0)),
        out_shape=sds((T, D), F32),
        scratch_shapes=[pltpu.VMEM((MLP_ROW_TILE, D), mm)],
        compiler_params=pltpu.CompilerParams(dimension_semantics=("parallel", "arbitrary"),
                                             vmem_limit_bytes=VMEM_LIMIT),
        name="mlp",
    )(x2, row(norm3_g), w_ff_in.astype(mm), w_ff_out.astype(mm))
    return x3.reshape(B, L, D)


def kernel(x, mem, norm1_g, w_in, dsa_cq_g, dsa_ckv_g, w_dsa_uq, w_dsa_ukv, w_idx_q, idx_k_ln_g, idx_k_ln_b, dsa_qn_g, dsa_kn_g, nsa_cmp_pe, w_nsa_cmp, nsa_qn_g, nsa_kn_g, w_out, norm2_g, mem_norm_g, w_xq, w_xk, w_xv, xq_norm_g, xk_norm_g, w_xo, norm3_g, w_ff_in, w_ff_out):
    params = (norm1_g, w_in, dsa_cq_g, dsa_ckv_g, w_dsa_uq, w_dsa_ukv, w_idx_q, idx_k_ln_g, idx_k_ln_b, dsa_qn_g,
              dsa_kn_g, nsa_cmp_pe, w_nsa_cmp, nsa_qn_g, nsa_kn_g, w_out, norm2_g, mem_norm_g, w_xq, w_xk, w_xv,
              xq_norm_g, xk_norm_g, w_xo, norm3_g, w_ff_in, w_ff_out)
    for layer in range(norm1_g.shape[0]):
        x = _layer(x, mem, *(p[layer] for p in params))
    return x
```

```python
import functools
import math

import jax
import jax.numpy as jnp
import numpy as np
from jax import lax
from jax.experimental import pallas as pl
from jax.experimental.pallas import tpu as pltpu

F32 = jnp.float32
I32 = jnp.int32
_MM_DTYPE = jnp.bfloat16

HEAD_DIM = 64
HALF_DIM = HEAD_DIM // 2
ROPE_THETA = 10000.0
EPS = 1e-6
ATTN_SCALE = HEAD_DIM ** -0.5
LOG2E = math.log2(math.e)

DSA_HEADS = 8
DSA_Q_RANK = 256
DSA_KV_RANK = 128
IDX_HEADS = 8
IDX_DIM = 64
IDX_SCALE = IDX_HEADS ** -0.5 * IDX_DIM ** -0.5
DSA_TOPK_MAX = 256

NSA_HEADS = 8
NSA_KV_HEADS = 2
NSA_GROUP = NSA_HEADS // NSA_KV_HEADS
CMP_BLOCK = 32
CMP_STRIDE = 16
SLC_BLOCK = 64
SLC_COUNT = 16
WINDOW = 512
FORCE_BONUS = 1e4

XATTN_HEADS = 4
XATTN_HEAD_DIM = 128

LANES = 128
SUBLANES = 8
INT_MIN = -(2 ** 31)
KEY_LOWEST_FINITE = INT_MIN + 0x00800000
F32_LOWEST = float(np.finfo(np.float32).min)
NEG_BIG = -1e30
VMEM_LIMIT = 56 * 1024 * 1024

ROW_TILE = 512
MLP_ROW_TILE = 1024
MLP_FF_TILE = 1024
DSA_Q_TILE = 256
NSA_Q_TILE = 128
KEY_TILE = 256
WIN_TILE = 128
V_AUG = 80
SEARCH_STRIDE = 4


def _mm(a, b):
    return jnp.dot(a.astype(_MM_DTYPE), b.astype(_MM_DTYPE), preferred_element_type=F32)


def _mm_nt(a, b):
    return lax.dot_general(a.astype(_MM_DTYPE), b.astype(_MM_DTYPE), (((1,), (1,)), ((), ())),
                           preferred_element_type=F32)


def _split(y):
    hi = y.astype(_MM_DTYPE)
    return hi, (y - hi.astype(F32)).astype(_MM_DTYPE)


def _split_mm(y, mat):
    hi, lo = _split(y)
    return jnp.dot(hi, mat, preferred_element_type=F32) + jnp.dot(lo, mat, preferred_element_type=F32)


def _split_mm_left(mat, y):
    hi, lo = _split(y)
    return jnp.dot(mat, hi, preferred_element_type=F32) + jnp.dot(mat, lo, preferred_element_type=F32)


def _rms_rows(x, g):
    return x * lax.rsqrt(jnp.mean(x * x, axis=-1, keepdims=True) + EPS) * g


def _head_rms(x, g, seg):
    return x * lax.rsqrt(_split_mm(x * x, seg) + EPS) * g


def _tile_lanes(t, n):
    return t if n == 1 else jnp.concatenate([t] * n, axis=1)


def _rope(y, cos_t, sin_t):
    w = y.shape[-1]
    lane = lax.broadcasted_iota(I32, y.shape, 1)
    swapped = jnp.where((lane & HALF_DIM) == 0, pltpu.roll(y, w - HALF_DIM, 1), pltpu.roll(y, HALF_DIM, 1))
    n = w // LANES
    return y * _tile_lanes(cos_t, n) + swapped * _tile_lanes(sin_t, n)


def _store_heads(ref, val):
    for hh in range(val.shape[1] // HEAD_DIM):
        ref[hh] = val[:, hh * HEAD_DIM:(hh + 1) * HEAD_DIM].astype(ref.dtype)


def _store_value_tiles(ref, v_t):
    tile = ref.shape[3]
    extra = ref.shape[2] - HEAD_DIM
    if extra:
        ones_row = jnp.where(lax.broadcasted_iota(I32, (extra, tile), 0) == 0, 1.0, 0.0).astype(ref.dtype)
    for g in range(ref.shape[0]):
        for k in range(ref.shape[1]):
            ref[g, k, 0:HEAD_DIM, :] = v_t[g * HEAD_DIM:(g + 1) * HEAD_DIM, k * tile:(k + 1) * tile].astype(ref.dtype)
            if extra:
                ref[g, k, HEAD_DIM:HEAD_DIM + extra, :] = ones_row


def _tree_sum(parts):
    while len(parts) > 1:
        parts = [parts[a] + parts[a + 1] for a in range(0, len(parts) - 1, 2)] + (
            [parts[-1]] if len(parts) % 2 else [])
    return parts[0]


def _memkv_kernel(mem_ref, g_ref, wk_ref, wv_ref, kg_ref, k_ref, v_ref):
    m = _rms_rows(mem_ref[0], g_ref[...])
    k = _mm(m, wk_ref[...])
    v = _mm(m, wv_ref[...])
    ks = []
    for h in range(XATTN_HEADS):
        ks.append(_rms_rows(k[:, h * XATTN_HEAD_DIM:(h + 1) * XATTN_HEAD_DIM], kg_ref[...]))
    k_ref[0] = jnp.concatenate(ks, axis=1).astype(k_ref.dtype)
    v_ref[0] = v.astype(v_ref.dtype)


def _prologue_kernel(x_ref, g1_ref, win_ref, cqg_ref, ckvg_ref, wuq_ref, wukv_ref, widx_ref,
                     lng_ref, lnb_ref, qng_ref, kng_ref, nqg_ref, nkg_ref, seg_ref, cos_ref, sin_ref,
                     qd_ref, qi_ref, qn_ref, ki_ref, kd_ref, vdt_ref, wi_ref, kc_ref, vc_ref,
                     ksl_ref, vslt_ref, kwn_ref, vwnt_ref, gate_ref):
    cos_t = cos_ref[...]
    sin_t = sin_ref[...]
    seg = seg_ref[...]
    seg1 = seg_ref[0:LANES, 0:LANES]
    q_scale = ATTN_SCALE * LOG2E

    h = _rms_rows(x_ref[...], g1_ref[...])
    proj = _mm(h, win_ref[...])

    c_q = _rms_rows(proj[:, 0:256], cqg_ref[...])
    c_kv = _rms_rows(proj[:, 256:384], ckvg_ref[...])

    q_d = _rope(_head_rms(_mm(c_q, wuq_ref[...]), qng_ref[...], seg), cos_t, sin_t) * q_scale
    _store_heads(qd_ref, q_d)
    q_i = _rope(_mm(c_q, widx_ref[...]), cos_t, sin_t)
    _store_heads(qi_ref, q_i)

    kv = _mm(c_kv, wukv_ref[...])
    k_d = _rope(_head_rms(kv, kng_ref[...], seg1), cos_t, sin_t)
    kd_ref[...] = k_d[:, 0:HEAD_DIM].astype(kd_ref.dtype)
    _store_value_tiles(vdt_ref, kv.T[HEAD_DIM:2 * HEAD_DIM, :])

    idx = proj[:, 384:512]
    mu = _split_mm(idx, seg1)
    d = idx - mu
    var = _split_mm(d * d, seg1)
    k_i = _rope(d * lax.rsqrt(var + EPS) * lng_ref[...] + lnb_ref[...], cos_t, sin_t)
    ki_ref[...] = k_i[:, 0:IDX_DIM].astype(ki_ref.dtype)
    wi_ref[...] = (idx * IDX_SCALE).T[IDX_DIM:IDX_DIM + IDX_HEADS, :]

    q_n = _rope(_head_rms(proj[:, 512:1024], nqg_ref[...], seg), cos_t, sin_t) * q_scale
    _store_heads(qn_ref, q_n)

    kc_ref[...] = proj[:, 1024:1152]
    vc_ref[...] = proj[:, 1152:1280]
    _store_heads(ksl_ref, _rope(_head_rms(proj[:, 1280:1408], nkg_ref[1:2, :], seg1), cos_t, sin_t))
    _store_value_tiles(vslt_ref, proj[:, 1408:1536].T)
    _store_heads(kwn_ref, _rope(_head_rms(proj[:, 1536:1664], nkg_ref[2:3, :], seg1), cos_t, sin_t))
    _store_value_tiles(vwnt_ref, proj[:, 1664:1792].T)
    gate_ref[...] = jax.nn.sigmoid(proj[:, 1792:1920]).T[0:gate_ref.shape[0], :]


def _compress_kernel(xk_ref, xv_ref, pe_ref, w_ref, g_ref, seg_ref, cos_ref, sin_ref, kc_ref, vc_ref):
    n = xk_ref.shape[1]

    def cmp_map(x, which):
        a = _mm(x + pe_ref[2 * which:2 * which + 1, :], w_ref[2 * which])
        b = _mm(x + pe_ref[2 * which + 1:2 * which + 2, :], w_ref[2 * which + 1])
        return a + pltpu.roll(b, n - 1, 0)

    k_c = cmp_map(xk_ref[0], 0)
    k_c = _rope(_head_rms(k_c, g_ref[...], seg_ref[...]), cos_ref[...], sin_ref[...])
    _store_heads(kc_ref.at[0], k_c)
    vc_ref[0] = cmp_map(xv_ref[0], 1).astype(vc_ref.dtype)


def _flash_pipelined(n_super, scores_fn, v_fn, s_buf, p_buf, acc_ref):
    n_lanes = acc_ref.shape[1]

    def softmax_step(slot, m):
        m_new = jnp.maximum(m, jnp.max(s_buf[slot], axis=0, keepdims=True))
        p_buf[slot] = jnp.exp2(s_buf[slot] - m_new).astype(p_buf.dtype)
        return m_new, jnp.exp2(m - m_new)

    def value_step(slot, jj, half, alpha):
        acc_ref[...] = alpha * acc_ref[...] + jnp.dot(v_fn(jj, half), p_buf[slot], preferred_element_type=F32)

    s_buf[0] = scores_fn(0, 0)
    p_buf[1] = jnp.zeros(p_buf.shape[1:], p_buf.dtype)
    acc_ref[...] = jnp.zeros(acc_ref.shape, F32)

    def body(jj, carry):
        m, alpha = carry
        s_buf[1] = scores_fn(jj, 1)
        m, alpha0 = softmax_step(0, m)
        value_step(1, jnp.maximum(jj - 1, 0), 1, alpha)
        s_buf[0] = scores_fn(jnp.minimum(jj + 1, n_super - 1), 0)
        m, alpha1 = softmax_step(1, m)
        value_step(0, jj, 0, alpha0)
        return m, alpha1

    init = (jnp.full((1, n_lanes), NEG_BIG, F32), jnp.ones((1, n_lanes), F32))
    _, alpha = lax.fori_loop(0, n_super, body, init)
    value_step(1, n_super - 1, 1, alpha)


def _normalised(acc_ref):
    return acc_ref[0:HEAD_DIM, :] / jnp.maximum(acc_ref[HEAD_DIM:HEAD_DIM + 1, :], 1e-30)


def _softmax_block(s, bias):
    s = s + bias
    mx = jnp.max(s, axis=0, keepdims=True)
    mx = jnp.where(mx > -jnp.inf, mx, 0.0)
    p = jnp.exp2(s - mx)
    return p, jnp.maximum(jnp.sum(p, axis=0, keepdims=True), 1e-30)


def _heads_to_rows(o_t, n_heads, tq):
    stacked = jnp.concatenate([o_t[:, h * tq:(h + 1) * tq] for h in range(n_heads)], axis=0)
    return stacked.T


def _key_to_float(key):
    bits = key ^ ((key >> 31) & jnp.int32(0x7FFFFFFF))
    return lax.bitcast_convert_type(bits, F32)


def _dsa_kernel(qi_ref, qd_ref, wi_ref, ki_ref, kd_ref, vt_ref, o_ref, sc_ref, s_buf, p_buf, acc_ref, *,
                k_sel, idx_bits):
    tq = qi_ref.shape[1]
    tk = vt_ref.shape[2]
    i = pl.program_id(1)
    q0 = i * tq
    n_super = (q0 + tq + 2 * tk - 1) // (2 * tk)
    n_kb = 2 * n_super
    t_row = q0 + lax.broadcasted_iota(I32, (1, tq), 1)
    w = wi_ref[...]
    q_i = qi_ref[...].reshape(IDX_HEADS * tq, IDX_DIM)
    q_d = qd_ref[...].reshape(DSA_HEADS * tq, HEAD_DIM)

    def block_rows(jj, half):
        return pl.multiple_of((2 * jj + half) * tk, tk)

    def idx_scores(jj, half):
        return _mm_nt(ki_ref[0, pl.ds(block_rows(jj, half), tk), :], q_i)

    def idx_finish(slot, jj, half):
        s = s_buf[slot]
        sc = jnp.maximum(s[:, 0:tq], 0.0) * w[0:1, :]
        for hh in range(1, IDX_HEADS):
            sc = sc + jnp.maximum(s[:, hh * tq:(hh + 1) * tq], 0.0) * w[hh:hh + 1, :]
        r0 = block_rows(jj, half)
        kpos = r0 + lax.broadcasted_iota(I32, (tk, 1), 0)
        sc_ref[pl.ds(r0, tk), :] = jnp.where(kpos <= t_row, sc, -jnp.inf)

    s_buf[0] = idx_scores(0, 0)

    def idx_body(jj, carry):
        s_buf[1] = idx_scores(jj, 1)
        idx_finish(0, jj, 0)
        s_buf[0] = idx_scores(jnp.minimum(jj + 1, n_super - 1), 0)
        idx_finish(1, jj, 1)
        return carry

    lax.fori_loop(0, n_super, idx_body, 0)

    def count(indicator):
        def body(c, cnt):
            for half in range(2):
                r0 = block_rows(c, half)
                rows = r0 + lax.broadcasted_iota(I32, (tk, 1), 0)
                f = indicator(sc_ref[pl.ds(r0, tk), :], rows)
                cnt = cnt + _tree_sum([f[r * SUBLANES:(r + 1) * SUBLANES] for r in range(tk // SUBLANES)])
            return cnt
        cnt = lax.fori_loop(0, n_super, body, jnp.zeros((SUBLANES, tq), F32))
        return jnp.sum(cnt, axis=0, keepdims=True)

    def bit_cond(state):
        return jnp.logical_and(state[0] < 32, state[1] == 0)

    def bit_pass(bi, state):
        thr, n_ge, settled = state
        cand = jnp.where(settled > 0.5, thr, thr + lax.shift_left(jnp.int32(1), 31 - bi))
        cand_f = _key_to_float(cand)
        tot = count(lambda v, rows: jnp.where(v >= cand_f, 1.0, 0.0))
        keep = tot >= k_sel
        return jnp.where(keep, cand, thr), jnp.where(keep, tot, n_ge), jnp.where(tot == k_sel, 1.0, settled)

    def bit_body(state):
        bi, _, thr, n_ge, settled = state
        thr, n_ge, settled = lax.fori_loop(bi, bi + SEARCH_STRIDE, bit_pass, (thr, n_ge, settled))
        return bi + SEARCH_STRIDE, (jnp.min(settled) > 0.5).astype(I32), thr, n_ge, settled

    _, _, thr, n_ge, _ = lax.while_loop(
        bit_cond, bit_body,
        (jnp.int32(0), jnp.int32(0), jnp.full((1, tq), INT_MIN, I32), jnp.zeros((1, tq), F32),
         jnp.zeros((1, tq), F32)))
    few = thr < jnp.int32(KEY_LOWEST_FINITE)
    thr_f = jnp.where(few, F32_LOWEST, _key_to_float(jnp.maximum(thr, jnp.int32(KEY_LOWEST_FINITE))))

    excess = jnp.where(few, 0.0, jnp.where(n_ge > k_sel, 1.0, 0.0))

    @pl.when(jnp.max(excess) > 0.5)
    def _():
        need = k_sel - count(lambda v, rows: jnp.where(v > thr_f, 1.0, 0.0))

        def jbit_body(bi, j0):
            cand = j0 + lax.shift_left(jnp.int32(1), idx_bits - 1 - bi)
            tot = count(lambda v, rows: jnp.where(rows < cand, jnp.where(v == thr_f, 1.0, 0.0), 0.0))
            return jnp.where(tot < need, cand, j0)
        j0 = lax.fori_loop(0, idx_bits, jbit_body, jnp.zeros((1, tq), I32))

        def demote(c, carry):
            r0 = pl.multiple_of(c * tk, tk)
            rows = r0 + lax.broadcasted_iota(I32, (tk, 1), 0)
            v = sc_ref[pl.ds(r0, tk), :]
            demoted = jnp.where(rows > j0, jnp.where(v == thr_f, -jnp.inf, v), v)
            sc_ref[pl.ds(r0, tk), :] = jnp.where(excess > 0.5, demoted, v)
            return carry
        lax.fori_loop(0, n_kb, demote, 0)

    def scores_fn(jj, half):
        r0 = block_rows(jj, half)
        sel = jnp.where(sc_ref[pl.ds(r0, tk), :] >= thr_f, 0.0, -jnp.inf)
        return _mm_nt(kd_ref[0, pl.ds(r0, tk), :], q_d) + _tile_lanes(sel, DSA_HEADS)

    _flash_pipelined(n_super, scores_fn, lambda jj, half: vt_ref[2 * jj + half], s_buf, p_buf, acc_ref)
    o_ref[...] = _heads_to_rows(_normalised(acc_ref), DSA_HEADS, tq).astype(o_ref.dtype)


def _nsa_kernel(qn_ref, kc_ref, vct_ref, ov_ref, ksl_ref, vslt_ref, kwn_ref, vwnt_ref, gate_ref, o_ref,
                sel_ref, s_buf, p_buf, acc_ref, out_ref, *, n_sel):
    tq = qn_ref.shape[1]
    n_cmp = kc_ref.shape[2]
    n_slc = ov_ref.shape[0]
    tk = vslt_ref.shape[3]
    tkw = vwnt_ref.shape[3]
    blk_per_tile = tk // SLC_BLOCK
    r_heads = NSA_GROUP
    nq = r_heads * tq
    win_keys = WINDOW + tq
    i = pl.program_id(1)
    q0 = i * tq
    t_row = q0 + lax.broadcasted_iota(I32, (1, tq), 1)
    gates = gate_ref[...]
    q_groups = [qn_ref[g * r_heads:(g + 1) * r_heads].reshape(nq, HEAD_DIM) for g in range(NSA_KV_HEADS)]

    def gate_wide(g, branch):
        return jnp.concatenate(
            [gates[(g * r_heads + r) * 3 + branch:(g * r_heads + r) * 3 + branch + 1, :] for r in range(r_heads)],
            axis=1)

    for g in range(NSA_KV_HEADS):
        lanes = slice(g * nq, (g + 1) * nq)

        cmp_end = lax.broadcasted_iota(I32, (n_cmp, 1), 0) * CMP_STRIDE + (CMP_BLOCK - 1)
        bias = _tile_lanes(jnp.where(cmp_end <= t_row, 0.0, -jnp.inf), r_heads)
        p, l = _softmax_block(_mm_nt(kc_ref[0, g], q_groups[g]), bias)
        p = p / l
        o_g = _mm(vct_ref[0, g], p) * gate_wide(g, 0)

        p_sum = p[:, 0:tq]
        for r in range(1, r_heads):
            p_sum = p_sum + p[:, r * tq:(r + 1) * tq]
        blk = _split_mm_left(ov_ref[...], p_sum)
        jrow = lax.broadcasted_iota(I32, (n_slc, 1), 0)
        jrow_f = jrow.astype(F32)
        tb = jnp.right_shift(t_row, SLC_BLOCK.bit_length() - 1)
        bonus = jnp.where(jrow == 0, FORCE_BONUS,
                          jnp.where(jrow == tb, FORCE_BONUS, jnp.where(jrow == tb - 1, FORCE_BONUS, 0.0)))
        val = jnp.where(jrow * SLC_BLOCK <= t_row, blk + bonus, -jnp.inf)
        sel_bias = jnp.full((n_slc, tq), -jnp.inf, F32)
        for _ in range(n_sel):
            top = jnp.max(val, axis=0, keepdims=True)
            first = jnp.min(jnp.where(val == top, jrow_f, float(n_slc)), axis=0, keepdims=True)
            pick = jrow_f == first
            sel_bias = jnp.where(pick, 0.0, sel_bias)
            val = jnp.where(pick, -jnp.inf, val)
        sel_ref[g] = sel_bias

        start = pl.multiple_of(jnp.maximum(q0 - WINDOW, 0), tkw)
        kpos = start + lax.broadcasted_iota(I32, (win_keys, 1), 0)
        bias = jnp.where(kpos <= t_row, jnp.where(kpos > t_row - WINDOW, 0.0, -jnp.inf), -jnp.inf)
        p, l = _softmax_block(_mm_nt(kwn_ref[g, pl.ds(start, win_keys), :], q_groups[g]),
                              _tile_lanes(bias, r_heads))
        o_w = None
        for k in range(win_keys // tkw):
            part = _mm(vwnt_ref[g, start // tkw + k], p[k * tkw:(k + 1) * tkw, :])
            o_w = part if o_w is None else o_w + part
        out_ref[:, lanes] = o_g + o_w / l * gate_wide(g, 2)

    n_super = (q0 + tq + 2 * tk - 1) // (2 * tk)
    for g in range(NSA_KV_HEADS):
        lanes = slice(g * nq, (g + 1) * nq)

        def scores_fn(jj, half, g=g):
            r0 = pl.multiple_of((2 * jj + half) * tk, tk)
            picked = sel_ref[g, pl.ds(pl.multiple_of(jj * 2 * blk_per_tile, 2 * blk_per_tile), 2 * blk_per_tile), :]
            blocks = [jnp.broadcast_to(picked[half * blk_per_tile + bb:half * blk_per_tile + bb + 1, :], (SLC_BLOCK, tq))
                      for bb in range(blk_per_tile)]
            kpos = r0 + lax.broadcasted_iota(I32, (tk, 1), 0)
            bias = jnp.where(kpos <= t_row, jnp.concatenate(blocks, axis=0), -jnp.inf)
            return _mm_nt(ksl_ref[g, pl.ds(r0, tk), :], q_groups[g]) + _tile_lanes(bias, r_heads)

        _flash_pipelined(n_super, scores_fn, lambda jj, half, g=g: vslt_ref[g, 2 * jj + half], s_buf, p_buf, acc_ref)
        out_ref[:, lanes] = out_ref[:, lanes] + _normalised(acc_ref) * gate_wide(g, 1)

    o_ref[...] = _heads_to_rows(out_ref[...], NSA_HEADS, tq).astype(o_ref.dtype)


def _post_kernel(x_ref, od_ref, on_ref, wo_ref, g2_ref, wq_ref, qg_ref, km_ref, vm_ref, wxo_ref, o_ref):
    half = od_ref.shape[1]
    x1 = x_ref[...] + _mm(od_ref[...], wo_ref[0:half, :]) + _mm(on_ref[...], wo_ref[half:2 * half, :])
    q = _mm(_rms_rows(x1, g2_ref[...]), wq_ref[...])
    km = km_ref[0]
    vm = vm_ref[0]
    heads = []
    for h in range(XATTN_HEADS):
        lo, hi = h * XATTN_HEAD_DIM, (h + 1) * XATTN_HEAD_DIM
        q_h = _rms_rows(q[:, lo:hi], qg_ref[...])
        s = _mm_nt(q_h, km[:, lo:hi]) * (XATTN_HEAD_DIM ** -0.5)
        p = jnp.exp(s - jnp.max(s, axis=-1, keepdims=True))
        heads.append(_mm(p, vm[:, lo:hi]) / jnp.sum(p, axis=-1, keepdims=True))
    o_ref[...] = x1 + _mm(jnp.concatenate(heads, axis=1), wxo_ref[...])


def _mlp_kernel(x_ref, g3_ref, w1_ref, w2_ref, o_ref, h_ref):
    @pl.when(pl.program_id(1) == 0)
    def _():
        x = x_ref[...]
        h_ref[...] = _rms_rows(x, g3_ref[...]).astype(h_ref.dtype)
        o_ref[...] = x

    a = jnp.maximum(jnp.dot(h_ref[...], w1_ref[...], preferred_element_type=F32), 0.0)
    o_ref[...] += _mm(a * a, w2_ref[...])


def _params(n_axes):
    return pltpu.CompilerParams(dimension_semantics=("parallel",) * n_axes, vmem_limit_bytes=VMEM_LIMIT)


def _full(shape):
    return pl.BlockSpec(shape, lambda *_: (0,) * len(shape))


def _rope_tables(pos):
    inv = ROPE_THETA ** (-jnp.arange(HALF_DIM, dtype=F32) / HALF_DIM)
    ang = pos.astype(F32)[:, None] * inv[None, :]
    cos, sin = jnp.cos(ang), jnp.sin(ang)
    reps = LANES // HEAD_DIM
    return (jnp.tile(jnp.concatenate([cos, cos], axis=1), (1, reps)),
            jnp.tile(jnp.concatenate([-sin, sin], axis=1), (1, reps)))


def _tile_gain(g, width):
    return jnp.tile(g.astype(F32), width // g.shape[0])[None, :]


def _layer(x, mem, norm1_g, w_in, dsa_cq_g, dsa_ckv_g, w_dsa_uq, w_dsa_ukv, w_idx_q, idx_k_ln_g, idx_k_ln_b,
           dsa_qn_g, dsa_kn_g, nsa_cmp_pe, w_nsa_cmp, nsa_qn_g, nsa_kn_g, w_out, norm2_g, mem_norm_g,
           w_xq, w_xk, w_xv, xq_norm_g, xk_norm_g, w_xo, norm3_g, w_ff_in, w_ff_out):
    B, L, D = x.shape
    M = mem.shape[1]
    mm = _MM_DTYPE
    T = B * L
    assert L % (2 * KEY_TILE) == 0 and ROW_TILE % KEY_TILE == 0 and L % ROW_TILE == 0 and T % MLP_ROW_TILE == 0 and L >= WINDOW + NSA_Q_TILE
    n_slc = L // SLC_BLOCK
    n_cmp_pad = L // CMP_STRIDE
    d_ff = w_ff_in.shape[1]
    xd = XATTN_HEADS * XATTN_HEAD_DIM
    G = NSA_KV_HEADS

    z = lambda n: jnp.zeros((D, n), F32)
    w_in_p = jnp.concatenate([
        w_in[:, 0:448], w_in[:, 448:456], z(56),
        w_in[:, 456:968],
        w_in[:, 968:1736],
        w_in[:, 1736:1760], z(104)], axis=1).astype(mm)
    seg_np = np.kron(np.eye(512 // HEAD_DIM, dtype=np.float32), np.full((HEAD_DIM, HEAD_DIM), 1.0 / HEAD_DIM, np.float32))
    seg = jnp.asarray(seg_np).astype(mm)
    cos_t, sin_t = _rope_tables(jnp.arange(L, dtype=jnp.int32))
    cos_c, sin_c = _rope_tables(jnp.arange(n_cmp_pad, dtype=jnp.int32) * CMP_STRIDE + (CMP_BLOCK - 1))
    pad64 = jnp.zeros((HEAD_DIM,), F32)
    ln_g = jnp.concatenate([idx_k_ln_g, pad64])[None, :]
    ln_b = jnp.concatenate([idx_k_ln_b, pad64])[None, :]
    nkg = jnp.tile(nsa_kn_g, (1, LANES // HEAD_DIM))
    row = lambda v: v.astype(F32)[None, :]
    sds = jax.ShapeDtypeStruct

    k_mem, v_mem = pl.pallas_call(
        _memkv_kernel,
        grid=(B,),
        in_specs=[pl.BlockSpec((1, M, D), lambda b: (b, 0, 0)), _full((1, D)), _full((D, xd)), _full((D, xd)),
                  _full((1, XATTN_HEAD_DIM))],
        out_specs=[pl.BlockSpec((1, M, xd), lambda b: (b, 0, 0))] * 2,
        out_shape=[sds((B, M, xd), mm)] * 2,
        compiler_params=_params(1), name="memkv",
    )(mem, row(mem_norm_g), w_xk.astype(mm), w_xv.astype(mm), row(xk_norm_g))

    n_rt = T // ROW_TILE
    rt_per_seq = L // ROW_TILE
    rows = lambda w: pl.BlockSpec((ROW_TILE, w), lambda i: (i, 0))
    heads = lambda n: pl.BlockSpec((n, ROW_TILE, HEAD_DIM), lambda i: (0, i, 0))
    table = pl.BlockSpec((ROW_TILE, LANES), lambda i: (i % rt_per_seq, 0))
    head_shape = lambda n: sds((n, T, HEAD_DIM), mm)
    win_per_row_tile = ROW_TILE // WIN_TILE
    keys_per_row_tile = ROW_TILE // KEY_TILE
    (qd, qi, qn, ki, kd, vd_t, wi_t, kc_raw, vc_raw, ksl, vsl_t, kwn, vwn_t, gate_t) = pl.pallas_call(
        _prologue_kernel,
        grid=(n_rt,),
        in_specs=[rows(D), _full((1, D)), _full(w_in_p.shape), _full((1, DSA_Q_RANK)), _full((1, DSA_KV_RANK)),
                  _full(w_dsa_uq.shape), _full(w_dsa_ukv.shape), _full(w_idx_q.shape),
                  _full((1, LANES)), _full((1, LANES)), _full((1, 512)), _full((1, LANES)), _full((1, 512)),
                  _full((3, LANES)), _full((512, 512)), table, table],
        out_specs=[heads(DSA_HEADS), heads(IDX_HEADS), heads(NSA_HEADS), rows(HEAD_DIM), rows(HEAD_DIM),
                   pl.BlockSpec((1, keys_per_row_tile, V_AUG, KEY_TILE), lambda i: (0, i, 0, 0)),
                   pl.BlockSpec((IDX_HEADS, ROW_TILE), lambda i: (0, i)), rows(LANES), rows(LANES),
                   heads(G), pl.BlockSpec((G, keys_per_row_tile, V_AUG, KEY_TILE), lambda i: (0, i, 0, 0)),
                   heads(G), pl.BlockSpec((G, win_per_row_tile, HEAD_DIM, WIN_TILE), lambda i: (0, i, 0, 0)),
                   pl.BlockSpec((NSA_HEADS * 3, ROW_TILE), lambda i: (0, i))],
        out_shape=[head_shape(DSA_HEADS), head_shape(IDX_HEADS), head_shape(NSA_HEADS), sds((T, HEAD_DIM), mm),
                   sds((T, HEAD_DIM), mm), sds((1, T // KEY_TILE, V_AUG, KEY_TILE), mm), sds((IDX_HEADS, T), F32),
                   sds((T, LANES), F32), sds((T, LANES), F32),
                   head_shape(G), sds((G, T // KEY_TILE, V_AUG, KEY_TILE), mm),
                   head_shape(G), sds((G, T // WIN_TILE, HEAD_DIM, WIN_TILE), mm),
                   sds((NSA_HEADS * 3, T), F32)],
        compiler_params=_params(1), name="prologue",
    )(x.reshape(T, D), row(norm1_g), w_in_p, row(dsa_cq_g), row(dsa_ckv_g), w_dsa_uq.astype(mm),
      w_dsa_ukv.astype(mm), w_idx_q.astype(mm), ln_g, ln_b, _tile_gain(dsa_qn_g, 512),
      _tile_gain(dsa_kn_g, LANES), _tile_gain(nsa_qn_g, 512), nkg, seg, cos_t, sin_t)

    half = CMP_BLOCK // 2
    eye_g = jnp.eye(G, dtype=F32)

    def cmp_weight(w):
        return jnp.einsum('lde,gh->lgdhe', w, eye_g).reshape(half * G * HEAD_DIM, G * HEAD_DIM)

    def cmp_pe(p):
        return jnp.broadcast_to(p[:, None, :], (half, G, HEAD_DIM)).reshape(1, -1)

    w_cmp = jnp.stack([cmp_weight(w_nsa_cmp[0, :half]), cmp_weight(w_nsa_cmp[0, half:]),
                       cmp_weight(w_nsa_cmp[1, :half]), cmp_weight(w_nsa_cmp[1, half:])]).astype(mm)
    pe_cmp = jnp.concatenate([cmp_pe(nsa_cmp_pe[0, :half]), cmp_pe(nsa_cmp_pe[0, half:]),
                              cmp_pe(nsa_cmp_pe[1, :half]), cmp_pe(nsa_cmp_pe[1, half:])], axis=0)
    cw = half * LANES
    k_c, v_c = pl.pallas_call(
        _compress_kernel,
        grid=(B,),
        in_specs=[pl.BlockSpec((1, n_cmp_pad, cw), lambda b: (b, 0, 0))] * 2
        + [_full((4, cw)), _full((4, cw, LANES)), _full((1, LANES)), _full((LANES, LANES)),
           _full((n_cmp_pad, LANES)), _full((n_cmp_pad, LANES))],
        out_specs=[pl.BlockSpec((1, G, n_cmp_pad, HEAD_DIM), lambda b: (b, 0, 0, 0)),
                   pl.BlockSpec((1, n_cmp_pad, LANES), lambda b: (b, 0, 0))],
        out_shape=[sds((B, G, n_cmp_pad, HEAD_DIM), mm), sds((B, n_cmp_pad, LANES), mm)],
        compiler_params=_params(1), name="compress",
    )(kc_raw.reshape(B, n_cmp_pad, cw), vc_raw.reshape(B, n_cmp_pad, cw), pe_cmp, w_cmp, nkg[0:1], seg[:LANES, :LANES],
      cos_c, sin_c)
    vc_t = v_c.reshape(B, n_cmp_pad, G, HEAD_DIM).transpose(0, 2, 3, 1)

    k_sel = min(DSA_TOPK_MAX, L // 4)
    per_seq = lambda shape: pl.BlockSpec((1,) + shape, lambda b, i: (b,) + (0,) * len(shape))
    seq_heads = pl.BlockSpec((G, L, HEAD_DIM), lambda b, i: (0, b, 0))

    def q_specs(tq):
        n_q = L // tq
        return (pl.BlockSpec((DSA_HEADS, tq, HEAD_DIM), lambda b, i: (0, b * n_q + i, 0)),
                lambda r: pl.BlockSpec((r, tq), lambda b, i: (0, b * n_q + i)),
                pl.BlockSpec((tq, DSA_HEADS * HEAD_DIM), lambda b, i: (b * n_q + i, 0)))

    tq = DSA_Q_TILE
    q_heads, q_cols, out_rows = q_specs(tq)
    o_dsa = pl.pallas_call(
        functools.partial(_dsa_kernel, k_sel=k_sel, idx_bits=int(L - 1).bit_length()),
        grid=(B, L // tq),
        in_specs=[q_heads, q_heads, q_cols(IDX_HEADS),
                  per_seq((L, IDX_DIM)), per_seq((L, HEAD_DIM)),
                  pl.BlockSpec((L // KEY_TILE, V_AUG, KEY_TILE), lambda b, i: (b, 0, 0))],
        out_specs=out_rows,
        out_shape=sds((T, DSA_HEADS * HEAD_DIM), mm),
        scratch_shapes=[pltpu.VMEM((L, tq), F32), pltpu.VMEM((2, KEY_TILE, DSA_HEADS * tq), F32),
                        pltpu.VMEM((2, KEY_TILE, DSA_HEADS * tq), mm),
                        pltpu.VMEM((V_AUG, DSA_HEADS * tq), F32)],
        compiler_params=_params(2), name="dsa",
    )(qi, qd, wi_t, ki.reshape(B, L, IDX_DIM), kd.reshape(B, L, HEAD_DIM), vd_t[0])

    ci = np.arange(n_cmp_pad)[None, :] * CMP_STRIDE
    sj = np.arange(n_slc)[:, None] * SLC_BLOCK
    ov_np = ((ci < sj + SLC_BLOCK) & (ci + CMP_BLOCK > sj) & (np.arange(n_cmp_pad)[None, :] < n_cmp_pad - 1))
    ov_t = jnp.asarray(ov_np.astype(np.float32)).astype(mm)
    tq = NSA_Q_TILE
    q_heads, q_cols, out_rows = q_specs(tq)
    o_nsa = pl.pallas_call(
        functools.partial(_nsa_kernel, n_sel=min(SLC_COUNT, n_slc)),
        grid=(B, L // tq),
        in_specs=[q_heads, per_seq((G, n_cmp_pad, HEAD_DIM)), per_seq((G, HEAD_DIM, n_cmp_pad)),
                  pl.BlockSpec((n_slc, n_cmp_pad), lambda b, i: (0, 0)),
                  seq_heads, pl.BlockSpec((G, L // KEY_TILE, V_AUG, KEY_TILE), lambda b, i: (0, b, 0, 0)),
                  seq_heads, pl.BlockSpec((G, L // WIN_TILE, HEAD_DIM, WIN_TILE), lambda b, i: (0, b, 0, 0)),
                  q_cols(NSA_HEADS * 3)],
        out_specs=out_rows,
        out_shape=sds((T, NSA_HEADS * HEAD_DIM), mm),
        scratch_shapes=[pltpu.VMEM((G, n_slc, tq), F32), pltpu.VMEM((2, KEY_TILE, NSA_GROUP * tq), F32),
                        pltpu.VMEM((2, KEY_TILE, NSA_GROUP * tq), mm),
                        pltpu.VMEM((V_AUG, NSA_GROUP * tq), F32),
                        pltpu.VMEM((HEAD_DIM, NSA_HEADS * tq), F32)],
        compiler_params=_params(2), name="nsa",
    )(qn, k_c, vc_t, ov_t, ksl, vsl_t, kwn, vwn_t, gate_t)

    x2 = pl.pallas_call(
        _post_kernel,
        grid=(n_rt,),
        in_specs=[rows(D), rows(512), rows(512), _full((D, D)), _full((1, D)), _full((D, xd)),
                  _full((1, XATTN_HEAD_DIM)),
                  pl.BlockSpec((1, M, xd), lambda i: (i // rt_per_seq, 0, 0)),
                  pl.BlockSpec((1, M, xd), lambda i: (i // rt_per_seq, 0, 0)), _full((xd, D))],
        out_specs=rows(D),
        out_shape=sds((T, D), F32),
        compiler_params=_params(1), name="post",
    )(x.reshape(T, D), o_dsa, o_nsa, w_out.astype(mm), row(norm2_g), w_xq.astype(mm), row(xq_norm_g), k_mem, v_mem,
      w_xo.astype(mm))

    x3 = pl.pallas_call(
        _mlp_kernel,
        grid=(T // MLP_ROW_TILE, d_ff // MLP_FF_TILE),
        in_specs=[pl.BlockSpec((MLP_ROW_TILE, D), lambda i, k: (i, 0)), pl.BlockSpec((1, D), lambda i, k: (0, 0)),
                  pl.BlockSpec((D, MLP_FF_TILE), lambda i, k: (0, k)),
                  pl.BlockSpec((MLP_FF_TILE, D), lambda i, k: (k, 0))],
        out_specs=pl.BlockSpec((MLP_ROW_TILE, D), lambda i, k: (i, 0)),
        out_shape=sds((T, D), F32),
        scratch_shapes=[pltpu.VMEM((MLP_ROW_TILE, D), mm)],
        compiler_params=pltpu.CompilerParams(dimension_semantics=("parallel", "arbitrary"),
                                             vmem_limit_bytes=VMEM_LIMIT),
        name="mlp",
    )(x2, row(norm3_g), w_ff_in.astype(mm), w_ff_out.astype(mm))
    return x3.reshape(B, L, D)


def kernel(x, mem, norm1_g, w_in, dsa_cq_g, dsa_ckv_g, w_dsa_uq, w_dsa_ukv, w_idx_q, idx_k_ln_g, idx_k_ln_b, dsa_qn_g, dsa_kn_g, nsa_cmp_pe, w_nsa_cmp, nsa_qn_g, nsa_kn_g, w_out, norm2_g, mem_norm_g, w_xq, w_xk, w_xv, xq_norm_g, xk_norm_g, w_xo, norm3_g, w_ff_in, w_ff_out):
    params = (norm1_g, w_in, dsa_cq_g, dsa_ckv_g, w_dsa_uq, w_dsa_ukv, w_idx_q, idx_k_ln_g, idx_k_ln_b, dsa_qn_g,
              dsa_kn_g, nsa_cmp_pe, w_nsa_cmp, nsa_qn_g, nsa_kn_g, w_out, norm2_g, mem_norm_g, w_xq, w_xk, w_xv,
              xq_norm_g, xk_norm_g, w_xo, norm3_g, w_ff_in, w_ff_out)
    for layer in range(norm1_g.shape[0]):
        x = _layer(x, mem, *(p[layer] for p in params))
    return x
```

```python
import functools
import math

import jax
import jax.numpy as jnp
import numpy as np
from jax import lax
from jax.experimental import pallas as pl
from jax.experimental.pallas import tpu as pltpu

F32 = jnp.float32
I32 = jnp.int32
_MM_DTYPE = jnp.bfloat16

HEAD_DIM = 64
HALF_DIM = HEAD_DIM // 2
ROPE_THETA = 10000.0
EPS = 1e-6
ATTN_SCALE = HEAD_DIM ** -0.5
LOG2E = math.log2(math.e)

DSA_HEADS = 8
DSA_Q_RANK = 256
DSA_KV_RANK = 128
IDX_HEADS = 8
IDX_DIM = 64
IDX_SCALE = IDX_HEADS ** -0.5 * IDX_DIM ** -0.5
DSA_TOPK_MAX = 256

NSA_HEADS = 8
NSA_KV_HEADS = 2
NSA_GROUP = NSA_HEADS // NSA_KV_HEADS
CMP_BLOCK = 32
CMP_STRIDE = 16
SLC_BLOCK = 64
SLC_COUNT = 16
WINDOW = 512
FORCE_BONUS = 1e4

XATTN_HEADS = 4
XATTN_HEAD_DIM = 128

LANES = 128
SUBLANES = 8
INT_MIN = -(2 ** 31)
KEY_LOWEST_FINITE = INT_MIN + 0x00800000
F32_LOWEST = float(np.finfo(np.float32).min)
NEG_BIG = -1e30
VMEM_LIMIT = 56 * 1024 * 1024

ROW_TILE = 1024
MLP_ROW_TILE = 1024
MLP_FF_TILE = 2048
DSA_Q_TILE = 256
NSA_Q_TILE = 128
KEY_TILE = 256
WIN_TILE = 128
V_AUG = 80
SEARCH_STRIDE = 4


def _mm(a, b):
    return jnp.dot(a.astype(_MM_DTYPE), b.astype(_MM_DTYPE), preferred_element_type=F32)


def _mm_nt(a, b):
    return lax.dot_general(a.astype(_MM_DTYPE), b.astype(_MM_DTYPE), (((1,), (1,)), ((), ())),
                           preferred_element_type=F32)


def _split(y):
    hi = y.astype(_MM_DTYPE)
    return hi, (y - hi.astype(F32)).astype(_MM_DTYPE)


def _split_mm(y, mat):
    hi, lo = _split(y)
    return jnp.dot(hi, mat, preferred_element_type=F32) + jnp.dot(lo, mat, preferred_element_type=F32)


def _split_mm_left(mat, y):
    hi, lo = _split(y)
    return jnp.dot(mat, hi, preferred_element_type=F32) + jnp.dot(mat, lo, preferred_element_type=F32)


def _rms_rows(x, g):
    return x * lax.rsqrt(jnp.mean(x * x, axis=-1, keepdims=True) + EPS) * g


def _head_rms(x, g, seg):
    return x * lax.rsqrt(_split_mm(x * x, seg) + EPS) * g


def _tile_lanes(t, n):
    return t if n == 1 else jnp.concatenate([t] * n, axis=1)


def _rope(y, cos_t, sin_t):
    w = y.shape[-1]
    lane = lax.broadcasted_iota(I32, y.shape, 1)
    swapped = jnp.where((lane & HALF_DIM) == 0, pltpu.roll(y, w - HALF_DIM, 1), pltpu.roll(y, HALF_DIM, 1))
    n = w // LANES
    return y * _tile_lanes(cos_t, n) + swapped * _tile_lanes(sin_t, n)


def _store_heads(ref, val):
    for hh in range(val.shape[1] // HEAD_DIM):
        ref[hh] = val[:, hh * HEAD_DIM:(hh + 1) * HEAD_DIM].astype(ref.dtype)


def _store_value_tiles(ref, v_t):
    tile = ref.shape[3]
    extra = ref.shape[2] - HEAD_DIM
    if extra:
        ones_row = jnp.where(lax.broadcasted_iota(I32, (extra, tile), 0) == 0, 1.0, 0.0).astype(ref.dtype)
    for g in range(ref.shape[0]):
        for k in range(ref.shape[1]):
            ref[g, k, 0:HEAD_DIM, :] = v_t[g * HEAD_DIM:(g + 1) * HEAD_DIM, k * tile:(k + 1) * tile].astype(ref.dtype)
            if extra:
                ref[g, k, HEAD_DIM:HEAD_DIM + extra, :] = ones_row


def _tree_sum(parts):
    while len(parts) > 1:
        parts = [parts[a] + parts[a + 1] for a in range(0, len(parts) - 1, 2)] + (
            [parts[-1]] if len(parts) % 2 else [])
    return parts[0]


def _memkv_kernel(mem_ref, g_ref, wk_ref, wv_ref, kg_ref, k_ref, v_ref):
    m = _rms_rows(mem_ref[0], g_ref[...])
    k = _mm(m, wk_ref[...])
    v = _mm(m, wv_ref[...])
    ks = []
    for h in range(XATTN_HEADS):
        ks.append(_rms_rows(k[:, h * XATTN_HEAD_DIM:(h + 1) * XATTN_HEAD_DIM], kg_ref[...]))
    k_ref[0] = jnp.concatenate(ks, axis=1).astype(k_ref.dtype)
    v_ref[0] = v.astype(v_ref.dtype)


def _prologue_kernel(x_ref, g1_ref, win_ref, cqg_ref, ckvg_ref, wuq_ref, wukv_ref, widx_ref,
                     lng_ref, lnb_ref, qng_ref, kng_ref, nqg_ref, nkg_ref, seg_ref, cos_ref, sin_ref,
                     qd_ref, qi_ref, qn_ref, ki_ref, kd_ref, vdt_ref, wi_ref, kc_ref, vc_ref,
                     ksl_ref, vslt_ref, kwn_ref, vwnt_ref, gate_ref):
    cos_t = cos_ref[...]
    sin_t = sin_ref[...]
    seg = seg_ref[...]
    seg1 = seg_ref[0:LANES, 0:LANES]
    q_scale = ATTN_SCALE * LOG2E

    h = _rms_rows(x_ref[...], g1_ref[...])
    proj = _mm(h, win_ref[...])

    c_q = _rms_rows(proj[:, 0:256], cqg_ref[...])
    c_kv = _rms_rows(proj[:, 256:384], ckvg_ref[...])

    q_d = _rope(_head_rms(_mm(c_q, wuq_ref[...]), qng_ref[...], seg), cos_t, sin_t) * q_scale
    _store_heads(qd_ref, q_d)
    q_i = _rope(_mm(c_q, widx_ref[...]), cos_t, sin_t)
    _store_heads(qi_ref, q_i)

    kv = _mm(c_kv, wukv_ref[...])
    k_d = _rope(_head_rms(kv, kng_ref[...], seg1), cos_t, sin_t)
    kd_ref[...] = k_d[:, 0:HEAD_DIM].astype(kd_ref.dtype)
    _store_value_tiles(vdt_ref, kv.T[HEAD_DIM:2 * HEAD_DIM, :])

    idx = proj[:, 384:512]
    mu = _split_mm(idx, seg1)
    d = idx - mu
    var = _split_mm(d * d, seg1)
    k_i = _rope(d * lax.rsqrt(var + EPS) * lng_ref[...] + lnb_ref[...], cos_t, sin_t)
    ki_ref[...] = k_i[:, 0:IDX_DIM].astype(ki_ref.dtype)
    wi_ref[...] = (idx * IDX_SCALE).T[IDX_DIM:IDX_DIM + IDX_HEADS, :]

    q_n = _rope(_head_rms(proj[:, 512:1024], nqg_ref[...], seg), cos_t, sin_t) * q_scale
    _store_heads(qn_ref, q_n)

    kc_ref[...] = proj[:, 1024:1152]
    vc_ref[...] = proj[:, 1152:1280]
    _store_heads(ksl_ref, _rope(_head_rms(proj[:, 1280:1408], nkg_ref[1:2, :], seg1), cos_t, sin_t))
    _store_value_tiles(vslt_ref, proj[:, 1408:1536].T)
    _store_heads(kwn_ref, _rope(_head_rms(proj[:, 1536:1664], nkg_ref[2:3, :], seg1), cos_t, sin_t))
    _store_value_tiles(vwnt_ref, proj[:, 1664:1792].T)
    gate_ref[...] = jax.nn.sigmoid(proj[:, 1792:1920]).T[0:gate_ref.shape[0], :]


def _compress_kernel(xk_ref, xv_ref, pe_ref, w_ref, g_ref, seg_ref, cos_ref, sin_ref, kc_ref, vc_ref):
    n = xk_ref.shape[1]

    def cmp_map(x, which):
        a = _mm(x + pe_ref[2 * which:2 * which + 1, :], w_ref[2 * which])
        b = _mm(x + pe_ref[2 * which + 1:2 * which + 2, :], w_ref[2 * which + 1])
        return a + pltpu.roll(b, n - 1, 0)

    k_c = cmp_map(xk_ref[0], 0)
    k_c = _rope(_head_rms(k_c, g_ref[...], seg_ref[...]), cos_ref[...], sin_ref[...])
    _store_heads(kc_ref.at[0], k_c)
    vc_ref[0] = cmp_map(xv_ref[0], 1).astype(vc_ref.dtype)


def _flash_pipelined(n_super, scores_fn, v_fn, s_buf, p_buf, acc_ref):
    n_lanes = acc_ref.shape[1]

    def softmax_step(slot, m):
        m_new = jnp.maximum(m, jnp.max(s_buf[slot], axis=0, keepdims=True))
        p_buf[slot] = jnp.exp2(s_buf[slot] - m_new).astype(p_buf.dtype)
        return m_new, jnp.exp2(m - m_new)

    def value_step(slot, jj, half, alpha):
        acc_ref[...] = alpha * acc_ref[...] + jnp.dot(v_fn(jj, half), p_buf[slot], preferred_element_type=F32)

    s_buf[0] = scores_fn(0, 0)
    p_buf[1] = jnp.zeros(p_buf.shape[1:], p_buf.dtype)
    acc_ref[...] = jnp.zeros(acc_ref.shape, F32)

    def body(jj, carry):
        m, alpha = carry
        s_buf[1] = scores_fn(jj, 1)
        m, alpha0 = softmax_step(0, m)
        value_step(1, jnp.maximum(jj - 1, 0), 1, alpha)
        s_buf[0] = scores_fn(jnp.minimum(jj + 1, n_super - 1), 0)
        m, alpha1 = softmax_step(1, m)
        value_step(0, jj, 0, alpha0)
        return m, alpha1

    init = (jnp.full((1, n_lanes), NEG_BIG, F32), jnp.ones((1, n_lanes), F32))
    _, alpha = lax.fori_loop(0, n_super, body, init)
    value_step(1, n_super - 1, 1, alpha)


def _normalised(acc_ref):
    return acc_ref[0:HEAD_DIM, :] / jnp.maximum(acc_ref[HEAD_DIM:HEAD_DIM + 1, :], 1e-30)


def _softmax_block(s, bias):
    s = s + bias
    mx = jnp.max(s, axis=0, keepdims=True)
    mx = jnp.where(mx > -jnp.inf, mx, 0.0)
    p = jnp.exp2(s - mx)
    return p, jnp.maximum(jnp.sum(p, axis=0, keepdims=True), 1e-30)


def _heads_to_rows(o_t, n_heads, tq):
    stacked = jnp.concatenate([o_t[:, h * tq:(h + 1) * tq] for h in range(n_heads)], axis=0)
    return stacked.T


def _key_to_float(key):
    bits = key ^ ((key >> 31) & jnp.int32(0x7FFFFFFF))
    return lax.bitcast_convert_type(bits, F32)


def _dsa_kernel(qi_ref, qd_ref, wi_ref, ki_ref, kd_ref, vt_ref, o_ref, sc_ref, s_buf, p_buf, acc_ref, *,
                k_sel, idx_bits):
    tq = qi_ref.shape[1]
    tk = vt_ref.shape[2]
    i = pl.program_id(1)
    q0 = i * tq
    n_super = (q0 + tq + 2 * tk - 1) // (2 * tk)
    n_kb = 2 * n_super
    t_row = q0 + lax.broadcasted_iota(I32, (1, tq), 1)
    w = wi_ref[...]
    q_i = qi_ref[...].reshape(IDX_HEADS * tq, IDX_DIM)
    q_d = qd_ref[...].reshape(DSA_HEADS * tq, HEAD_DIM)

    def block_rows(jj, half):
        return pl.multiple_of((2 * jj + half) * tk, tk)

    def idx_scores(jj, half):
        return _mm_nt(ki_ref[0, pl.ds(block_rows(jj, half), tk), :], q_i)

    def idx_finish(slot, jj, half):
        s = s_buf[slot]
        sc = jnp.maximum(s[:, 0:tq], 0.0) * w[0:1, :]
        for hh in range(1, IDX_HEADS):
            sc = sc + jnp.maximum(s[:, hh * tq:(hh + 1) * tq], 0.0) * w[hh:hh + 1, :]
        r0 = block_rows(jj, half)
        kpos = r0 + lax.broadcasted_iota(I32, (tk, 1), 0)
        sc_ref[pl.ds(r0, tk), :] = jnp.where(kpos <= t_row, sc, -jnp.inf)

    s_buf[0] = idx_scores(0, 0)

    def idx_body(jj, carry):
        s_buf[1] = idx_scores(jj, 1)
        idx_finish(0, jj, 0)
        s_buf[0] = idx_scores(jnp.minimum(jj + 1, n_super - 1), 0)
        idx_finish(1, jj, 1)
        return carry

    lax.fori_loop(0, n_super, idx_body, 0)

    def count(indicator):
        def body(c, cnt):
            for half in range(2):
                r0 = block_rows(c, half)
                rows = r0 + lax.broadcasted_iota(I32, (tk, 1), 0)
                f = indicator(sc_ref[pl.ds(r0, tk), :], rows)
                cnt = cnt + _tree_sum([f[r * SUBLANES:(r + 1) * SUBLANES] for r in range(tk // SUBLANES)])
            return cnt
        cnt = lax.fori_loop(0, n_super, body, jnp.zeros((SUBLANES, tq), F32))
        return jnp.sum(cnt, axis=0, keepdims=True)

    def bit_cond(state):
        return jnp.logical_and(state[0] < 32, state[1] == 0)

    def bit_pass(bi, state):
        thr, n_ge, settled = state
        cand = jnp.where(settled > 0.5, thr, thr + lax.shift_left(jnp.int32(1), 31 - bi))
        cand_f = _key_to_float(cand)
        tot = count(lambda v, rows: jnp.where(v >= cand_f, 1.0, 0.0))
        keep = tot >= k_sel
        return jnp.where(keep, cand, thr), jnp.where(keep, tot, n_ge), jnp.where(tot == k_sel, 1.0, settled)

    def bit_body(state):
        bi, _, thr, n_ge, settled = state
        thr, n_ge, settled = lax.fori_loop(bi, bi + SEARCH_STRIDE, bit_pass, (thr, n_ge, settled))
        return bi + SEARCH_STRIDE, (jnp.min(settled) > 0.5).astype(I32), thr, n_ge, settled

    _, _, thr, n_ge, _ = lax.while_loop(
        bit_cond, bit_body,
        (jnp.int32(0), jnp.int32(0), jnp.full((1, tq), INT_MIN, I32), jnp.zeros((1, tq), F32),
         jnp.zeros((1, tq), F32)))
    few = thr < jnp.int32(KEY_LOWEST_FINITE)
    thr_f = jnp.where(few, F32_LOWEST, _key_to_float(jnp.maximum(thr, jnp.int32(KEY_LOWEST_FINITE))))

    excess = jnp.where(few, 0.0, jnp.where(n_ge > k_sel, 1.0, 0.0))

    @pl.when(jnp.max(excess) > 0.5)
    def _():
        need = k_sel - count(lambda v, rows: jnp.where(v > thr_f, 1.0, 0.0))

        def jbit_body(bi, j0):
            cand = j0 + lax.shift_left(jnp.int32(1), idx_bits - 1 - bi)
            tot = count(lambda v, rows: jnp.where(rows < cand, jnp.where(v == thr_f, 1.0, 0.0), 0.0))
            return jnp.where(tot < need, cand, j0)
        j0 = lax.fori_loop(0, idx_bits, jbit_body, jnp.zeros((1, tq), I32))

        def demote(c, carry):
            r0 = pl.multiple_of(c * tk, tk)
            rows = r0 + lax.broadcasted_iota(I32, (tk, 1), 0)
            v = sc_ref[pl.ds(r0, tk), :]
            demoted = jnp.where(rows > j0, jnp.where(v == thr_f, -jnp.inf, v), v)
            sc_ref[pl.ds(r0, tk), :] = jnp.where(excess > 0.5, demoted, v)
            return carry
        lax.fori_loop(0, n_kb, demote, 0)

    def scores_fn(jj, half):
        r0 = block_rows(jj, half)
        sel = jnp.where(sc_ref[pl.ds(r0, tk), :] >= thr_f, 0.0, -jnp.inf)
        return _mm_nt(kd_ref[0, pl.ds(r0, tk), :], q_d) + _tile_lanes(sel, DSA_HEADS)

    _flash_pipelined(n_super, scores_fn, lambda jj, half: vt_ref[2 * jj + half], s_buf, p_buf, acc_ref)
    o_ref[...] = _heads_to_rows(_normalised(acc_ref), DSA_HEADS, tq).astype(o_ref.dtype)


def _nsa_kernel(qn_ref, kc_ref, vct_ref, ov_ref, ksl_ref, vslt_ref, kwn_ref, vwnt_ref, gate_ref, o_ref,
                sel_ref, s_buf, p_buf, acc_ref, out_ref, *, n_sel):
    tq = qn_ref.shape[1]
    n_cmp = kc_ref.shape[2]
    n_slc = ov_ref.shape[0]
    tk = vslt_ref.shape[3]
    tkw = vwnt_ref.shape[3]
    blk_per_tile = tk // SLC_BLOCK
    r_heads = NSA_GROUP
    nq = r_heads * tq
    win_keys = WINDOW + tq
    i = pl.program_id(1)
    q0 = i * tq
    t_row = q0 + lax.broadcasted_iota(I32, (1, tq), 1)
    gates = gate_ref[...]
    q_groups = [qn_ref[g * r_heads:(g + 1) * r_heads].reshape(nq, HEAD_DIM) for g in range(NSA_KV_HEADS)]

    def gate_wide(g, branch):
        return jnp.concatenate(
            [gates[(g * r_heads + r) * 3 + branch:(g * r_heads + r) * 3 + branch + 1, :] for r in range(r_heads)],
            axis=1)

    for g in range(NSA_KV_HEADS):
        lanes = slice(g * nq, (g + 1) * nq)

        cmp_end = lax.broadcasted_iota(I32, (n_cmp, 1), 0) * CMP_STRIDE + (CMP_BLOCK - 1)
        bias = _tile_lanes(jnp.where(cmp_end <= t_row, 0.0, -jnp.inf), r_heads)
        p, l = _softmax_block(_mm_nt(kc_ref[0, g], q_groups[g]), bias)
        p = p / l
        o_g = _mm(vct_ref[0, g], p) * gate_wide(g, 0)

        p_sum = p[:, 0:tq]
        for r in range(1, r_heads):
            p_sum = p_sum + p[:, r * tq:(r + 1) * tq]
        blk = _split_mm_left(ov_ref[...], p_sum)
        jrow = lax.broadcasted_iota(I32, (n_slc, 1), 0)
        jrow_f = jrow.astype(F32)
        tb = jnp.right_shift(t_row, SLC_BLOCK.bit_length() - 1)
        bonus = jnp.where(jrow == 0, FORCE_BONUS,
                          jnp.where(jrow == tb, FORCE_BONUS, jnp.where(jrow == tb - 1, FORCE_BONUS, 0.0)))
        val = jnp.where(jrow * SLC_BLOCK <= t_row, blk + bonus, -jnp.inf)
        sel_bias = jnp.full((n_slc, tq), -jnp.inf, F32)
        for _ in range(n_sel):
            top = jnp.max(val, axis=0, keepdims=True)
            first = jnp.min(jnp.where(val == top, jrow_f, float(n_slc)), axis=0, keepdims=True)
            pick = jrow_f == first
            sel_bias = jnp.where(pick, 0.0, sel_bias)
            val = jnp.where(pick, -jnp.inf, val)
        sel_ref[g] = sel_bias

        start = pl.multiple_of(jnp.maximum(q0 - WINDOW, 0), tkw)
        kpos = start + lax.broadcasted_iota(I32, (win_keys, 1), 0)
        bias = jnp.where(kpos <= t_row, jnp.where(kpos > t_row - WINDOW, 0.0, -jnp.inf), -jnp.inf)
        p, l = _softmax_block(_mm_nt(kwn_ref[g, pl.ds(start, win_keys), :], q_groups[g]),
                              _tile_lanes(bias, r_heads))
        o_w = None
        for k in range(win_keys // tkw):
            part = _mm(vwnt_ref[g, start // tkw + k], p[k * tkw:(k + 1) * tkw, :])
            o_w = part if o_w is None else o_w + part
        out_ref[:, lanes] = o_g + o_w / l * gate_wide(g, 2)

    n_super = (q0 + tq + 2 * tk - 1) // (2 * tk)
    for g in range(NSA_KV_HEADS):
        lanes = slice(g * nq, (g + 1) * nq)

        def scores_fn(jj, half, g=g):
            r0 = pl.multiple_of((2 * jj + half) * tk, tk)
            picked = sel_ref[g, pl.ds(pl.multiple_of(jj * 2 * blk_per_tile, 2 * blk_per_tile), 2 * blk_per_tile), :]
            blocks = [jnp.broadcast_to(picked[half * blk_per_tile + bb:half * blk_per_tile + bb + 1, :], (SLC_BLOCK, tq))
                      for bb in range(blk_per_tile)]
            kpos = r0 + lax.broadcasted_iota(I32, (tk, 1), 0)
            bias = jnp.where(kpos <= t_row, jnp.concatenate(blocks, axis=0), -jnp.inf)
            return _mm_nt(ksl_ref[g, pl.ds(r0, tk), :], q_groups[g]) + _tile_lanes(bias, r_heads)

        _flash_pipelined(n_super, scores_fn, lambda jj, half, g=g: vslt_ref[g, 2 * jj + half], s_buf, p_buf, acc_ref)
        out_ref[:, lanes] = out_ref[:, lanes] + _normalised(acc_ref) * gate_wide(g, 1)

    o_ref[...] = _heads_to_rows(out_ref[...], NSA_HEADS, tq).astype(o_ref.dtype)


def _post_kernel(x_ref, od_ref, on_ref, wo_ref, g2_ref, wq_ref, qg_ref, km_ref, vm_ref, wxo_ref, o_ref):
    half = od_ref.shape[1]
    x1 = x_ref[...] + _mm(od_ref[...], wo_ref[0:half, :]) + _mm(on_ref[...], wo_ref[half:2 * half, :])
    q = _mm(_rms_rows(x1, g2_ref[...]), wq_ref[...])
    km = km_ref[0]
    vm = vm_ref[0]
    heads = []
    for h in range(XATTN_HEADS):
        lo, hi = h * XATTN_HEAD_DIM, (h + 1) * XATTN_HEAD_DIM
        q_h = _rms_rows(q[:, lo:hi], qg_ref[...])
        s = _mm_nt(q_h, km[:, lo:hi]) * (XATTN_HEAD_DIM ** -0.5)
        p = jnp.exp(s - jnp.max(s, axis=-1, keepdims=True))
        heads.append(_mm(p, vm[:, lo:hi]) / jnp.sum(p, axis=-1, keepdims=True))
    o_ref[...] = x1 + _mm(jnp.concatenate(heads, axis=1), wxo_ref[...])


def _mlp_kernel(x_ref, g3_ref, w1_ref, w2_ref, o_ref, h_ref):
    @pl.when(pl.program_id(1) == 0)
    def _():
        x = x_ref[...]
        h_ref[...] = _rms_rows(x, g3_ref[...]).astype(h_ref.dtype)
        o_ref[...] = x

    a = jnp.maximum(jnp.dot(h_ref[...], w1_ref[...], preferred_element_type=F32), 0.0)
    o_ref[...] += _mm(a * a, w2_ref[...])


def _params(n_axes):
    return pltpu.CompilerParams(dimension_semantics=("parallel",) * n_axes, vmem_limit_bytes=VMEM_LIMIT)


def _full(shape):
    return pl.BlockSpec(shape, lambda *_: (0,) * len(shape))


def _rope_tables(pos):
    inv = ROPE_THETA ** (-jnp.arange(HALF_DIM, dtype=F32) / HALF_DIM)
    ang = pos.astype(F32)[:, None] * inv[None, :]
    cos, sin = jnp.cos(ang), jnp.sin(ang)
    reps = LANES // HEAD_DIM
    return (jnp.tile(jnp.concatenate([cos, cos], axis=1), (1, reps)),
            jnp.tile(jnp.concatenate([-sin, sin], axis=1), (1, reps)))


def _tile_gain(g, width):
    return jnp.tile(g.astype(F32), width // g.shape[0])[None, :]


def _layer(x, mem, norm1_g, w_in, dsa_cq_g, dsa_ckv_g, w_dsa_uq, w_dsa_ukv, w_idx_q, idx_k_ln_g, idx_k_ln_b,
           dsa_qn_g, dsa_kn_g, nsa_cmp_pe, w_nsa_cmp, nsa_qn_g, nsa_kn_g, w_out, norm2_g, mem_norm_g,
           w_xq, w_xk, w_xv, xq_norm_g, xk_norm_g, w_xo, norm3_g, w_ff_in, w_ff_out):
    B, L, D = x.shape
    M = mem.shape[1]
    mm = _MM_DTYPE
    T = B * L
    assert L % (2 * KEY_TILE) == 0 and ROW_TILE % KEY_TILE == 0 and L % ROW_TILE == 0 and T % MLP_ROW_TILE == 0 and L >= WINDOW + NSA_Q_TILE
    n_slc = L // SLC_BLOCK
    n_cmp_pad = L // CMP_STRIDE
    d_ff = w_ff_in.shape[1]
    xd = XATTN_HEADS * XATTN_HEAD_DIM
    G = NSA_KV_HEADS

    z = lambda n: jnp.zeros((D, n), F32)
    w_in_p = jnp.concatenate([
        w_in[:, 0:448], w_in[:, 448:456], z(56),
        w_in[:, 456:968],
        w_in[:, 968:1736],
        w_in[:, 1736:1760], z(104)], axis=1).astype(mm)
    seg_np = np.kron(np.eye(512 // HEAD_DIM, dtype=np.float32), np.full((HEAD_DIM, HEAD_DIM), 1.0 / HEAD_DIM, np.float32))
    seg = jnp.asarray(seg_np).astype(mm)
    cos_t, sin_t = _rope_tables(jnp.arange(L, dtype=jnp.int32))
    cos_c, sin_c = _rope_tables(jnp.arange(n_cmp_pad, dtype=jnp.int32) * CMP_STRIDE + (CMP_BLOCK - 1))
    pad64 = jnp.zeros((HEAD_DIM,), F32)
    ln_g = jnp.concatenate([idx_k_ln_g, pad64])[None, :]
    ln_b = jnp.concatenate([idx_k_ln_b, pad64])[None, :]
    nkg = jnp.tile(nsa_kn_g, (1, LANES // HEAD_DIM))
    row = lambda v: v.astype(F32)[None, :]
    sds = jax.ShapeDtypeStruct

    k_mem, v_mem = pl.pallas_call(
        _memkv_kernel,
        grid=(B,),
        in_specs=[pl.BlockSpec((1, M, D), lambda b: (b, 0, 0)), _full((1, D)), _full((D, xd)), _full((D, xd)),
                  _full((1, XATTN_HEAD_DIM))],
        out_specs=[pl.BlockSpec((1, M, xd), lambda b: (b, 0, 0))] * 2,
        out_shape=[sds((B, M, xd), mm)] * 2,
        compiler_params=_params(1), name="memkv",
    )(mem, row(mem_norm_g), w_xk.astype(mm), w_xv.astype(mm), row(xk_norm_g))

    n_rt = T // ROW_TILE
    rt_per_seq = L // ROW_TILE
    rows = lambda w: pl.BlockSpec((ROW_TILE, w), lambda i: (i, 0))
    heads = lambda n: pl.BlockSpec((n, ROW_TILE, HEAD_DIM), lambda i: (0, i, 0))
    table = pl.BlockSpec((ROW_TILE, LANES), lambda i: (i % rt_per_seq, 0))
    head_shape = lambda n: sds((n, T, HEAD_DIM), mm)
    win_per_row_tile = ROW_TILE // WIN_TILE
    keys_per_row_tile = ROW_TILE // KEY_TILE
    (qd, qi, qn, ki, kd, vd_t, wi_t, kc_raw, vc_raw, ksl, vsl_t, kwn, vwn_t, gate_t) = pl.pallas_call(
        _prologue_kernel,
        grid=(n_rt,),
        in_specs=[rows(D), _full((1, D)), _full(w_in_p.shape), _full((1, DSA_Q_RANK)), _full((1, DSA_KV_RANK)),
                  _full(w_dsa_uq.shape), _full(w_dsa_ukv.shape), _full(w_idx_q.shape),
                  _full((1, LANES)), _full((1, LANES)), _full((1, 512)), _full((1, LANES)), _full((1, 512)),
                  _full((3, LANES)), _full((512, 512)), table, table],
        out_specs=[heads(DSA_HEADS), heads(IDX_HEADS), heads(NSA_HEADS), rows(HEAD_DIM), rows(HEAD_DIM),
                   pl.BlockSpec((1, keys_per_row_tile, V_AUG, KEY_TILE), lambda i: (0, i, 0, 0)),
                   pl.BlockSpec((IDX_HEADS, ROW_TILE), lambda i: (0, i)), rows(LANES), rows(LANES),
                   heads(G), pl.BlockSpec((G, keys_per_row_tile, V_AUG, KEY_TILE), lambda i: (0, i, 0, 0)),
                   heads(G), pl.BlockSpec((G, win_per_row_tile, HEAD_DIM, WIN_TILE), lambda i: (0, i, 0, 0)),
                   pl.BlockSpec((NSA_HEADS * 3, ROW_TILE), lambda i: (0, i))],
        out_shape=[head_shape(DSA_HEADS), head_shape(IDX_HEADS), head_shape(NSA_HEADS), sds((T, HEAD_DIM), mm),
                   sds((T, HEAD_DIM), mm), sds((1, T // KEY_TILE, V_AUG, KEY_TILE), mm), sds((IDX_HEADS, T), F32),
                   sds((T, LANES), F32), sds((T, LANES), F32),
                   head_shape(G), sds((G, T // KEY_TILE, V_AUG, KEY_TILE), mm),
                   head_shape(G), sds((G, T // WIN_TILE, HEAD_DIM, WIN_TILE), mm),
                   sds((NSA_HEADS * 3, T), F32)],
        compiler_params=_params(1), name="prologue",
    )(x.reshape(T, D), row(norm1_g), w_in_p, row(dsa_cq_g), row(dsa_ckv_g), w_dsa_uq.astype(mm),
      w_dsa_ukv.astype(mm), w_idx_q.astype(mm), ln_g, ln_b, _tile_gain(dsa_qn_g, 512),
      _tile_gain(dsa_kn_g, LANES), _tile_gain(nsa_qn_g, 512), nkg, seg, cos_t, sin_t)

    half = CMP_BLOCK // 2
    eye_g = jnp.eye(G, dtype=F32)

    def cmp_weight(w):
        return jnp.einsum('lde,gh->lgdhe', w, eye_g).reshape(half * G * HEAD_DIM, G * HEAD_DIM)

    def cmp_pe(p):
        return jnp.broadcast_to(p[:, None, :], (half, G, HEAD_DIM)).reshape(1, -1)

    w_cmp = jnp.stack([cmp_weight(w_nsa_cmp[0, :half]), cmp_weight(w_nsa_cmp[0, half:]),
                       cmp_weight(w_nsa_cmp[1, :half]), cmp_weight(w_nsa_cmp[1, half:])]).astype(mm)
    pe_cmp = jnp.concatenate([cmp_pe(nsa_cmp_pe[0, :half]), cmp_pe(nsa_cmp_pe[0, half:]),
                              cmp_pe(nsa_cmp_pe[1, :half]), cmp_pe(nsa_cmp_pe[1, half:])], axis=0)
    cw = half * LANES
    k_c, v_c = pl.pallas_call(
        _compress_kernel,
        grid=(B,),
        in_specs=[pl.BlockSpec((1, n_cmp_pad, cw), lambda b: (b, 0, 0))] * 2
        + [_full((4, cw)), _full((4, cw, LANES)), _full((1, LANES)), _full((LANES, LANES)),
           _full((n_cmp_pad, LANES)), _full((n_cmp_pad, LANES))],
        out_specs=[pl.BlockSpec((1, G, n_cmp_pad, HEAD_DIM), lambda b: (b, 0, 0, 0)),
                   pl.BlockSpec((1, n_cmp_pad, LANES), lambda b: (b, 0, 0))],
        out_shape=[sds((B, G, n_cmp_pad, HEAD_DIM), mm), sds((B, n_cmp_pad, LANES), mm)],
        compiler_params=_params(1), name="compress",
    )(kc_raw.reshape(B, n_cmp_pad, cw), vc_raw.reshape(B, n_cmp_pad, cw), pe_cmp, w_cmp, nkg[0:1], seg[:LANES, :LANES],
      cos_c, sin_c)
    vc_t = v_c.reshape(B, n_cmp_pad, G, HEAD_DIM).transpose(0, 2, 3, 1)

    k_sel = min(DSA_TOPK_MAX, L // 4)
    per_seq = lambda shape: pl.BlockSpec((1,) + shape, lambda b, i: (b,) + (0,) * len(shape))
    seq_heads = pl.BlockSpec((G, L, HEAD_DIM), lambda b, i: (0, b, 0))

    def q_specs(tq):
        n_q = L // tq
        return (pl.BlockSpec((DSA_HEADS, tq, HEAD_DIM), lambda b, i: (0, b * n_q + i, 0)),
                lambda r: pl.BlockSpec((r, tq), lambda b, i: (0, b * n_q + i)),
                pl.BlockSpec((tq, DSA_HEADS * HEAD_DIM), lambda b, i: (b * n_q + i, 0)))

    tq = DSA_Q_TILE
    q_heads, q_cols, out_rows = q_specs(tq)
    o_dsa = pl.pallas_call(
        functools.partial(_dsa_kernel, k_sel=k_sel, idx_bits=int(L - 1).bit_length()),
        grid=(B, L // tq),
        in_specs=[q_heads, q_heads, q_cols(IDX_HEADS),
                  per_seq((L, IDX_DIM)), per_seq((L, HEAD_DIM)),
                  pl.BlockSpec((L // KEY_TILE, V_AUG, KEY_TILE), lambda b, i: (b, 0, 0))],
        out_specs=out_rows,
        out_shape=sds((T, DSA_HEADS * HEAD_DIM), mm),
        scratch_shapes=[pltpu.VMEM((L, tq), F32), pltpu.VMEM((2, KEY_TILE, DSA_HEADS * tq), F32),
                        pltpu.VMEM((2, KEY_TILE, DSA_HEADS * tq), mm),
                        pltpu.VMEM((V_AUG, DSA_HEADS * tq), F32)],
        compiler_params=_params(2), name="dsa",
    )(qi, qd, wi_t, ki.reshape(B, L, IDX_DIM), kd.reshape(B, L, HEAD_DIM), vd_t[0])

    ci = np.arange(n_cmp_pad)[None, :] * CMP_STRIDE
    sj = np.arange(n_slc)[:, None] * SLC_BLOCK
    ov_np = ((ci < sj + SLC_BLOCK) & (ci + CMP_BLOCK > sj) & (np.arange(n_cmp_pad)[None, :] < n_cmp_pad - 1))
    ov_t = jnp.asarray(ov_np.astype(np.float32)).astype(mm)
    tq = NSA_Q_TILE
    q_heads, q_cols, out_rows = q_specs(tq)
    o_nsa = pl.pallas_call(
        functools.partial(_nsa_kernel, n_sel=min(SLC_COUNT, n_slc)),
        grid=(B, L // tq),
        in_specs=[q_heads, per_seq((G, n_cmp_pad, HEAD_DIM)), per_seq((G, HEAD_DIM, n_cmp_pad)),
                  pl.BlockSpec((n_slc, n_cmp_pad), lambda b, i: (0, 0)),
                  seq_heads, pl.BlockSpec((G, L // KEY_TILE, V_AUG, KEY_TILE), lambda b, i: (0, b, 0, 0)),
                  seq_heads, pl.BlockSpec((G, L // WIN_TILE, HEAD_DIM, WIN_TILE), lambda b, i: (0, b, 0, 0)),
                  q_cols(NSA_HEADS * 3)],
        out_specs=out_rows,
        out_shape=sds((T, NSA_HEADS * HEAD_DIM), mm),
        scratch_shapes=[pltpu.VMEM((G, n_slc, tq), F32), pltpu.VMEM((2, KEY_TILE, NSA_GROUP * tq), F32),
                        pltpu.VMEM((2, KEY_TILE, NSA_GROUP * tq), mm),
                        pltpu.VMEM((V_AUG, NSA_GROUP * tq), F32),
                        pltpu.VMEM((HEAD_DIM, NSA_HEADS * tq), F32)],
        compiler_params=_params(2), name="nsa",
    )(qn, k_c, vc_t, ov_t, ksl, vsl_t, kwn, vwn_t, gate_t)

    x2 = pl.pallas_call(
        _post_kernel,
        grid=(n_rt,),
        in_specs=[rows(D), rows(512), rows(512), _full((D, D)), _full((1, D)), _full((D, xd)),
                  _full((1, XATTN_HEAD_DIM)),
                  pl.BlockSpec((1, M, xd), lambda i: (i // rt_per_seq, 0, 0)),
                  pl.BlockSpec((1, M, xd), lambda i: (i // rt_per_seq, 0, 0)), _full((xd, D))],
        out_specs=rows(D),
        out_shape=sds((T, D), F32),
        compiler_params=_params(1), name="post",
    )(x.reshape(T, D), o_dsa, o_nsa, w_out.astype(mm), row(norm2_g), w_xq.astype(mm), row(xq_norm_g), k_mem, v_mem,
      w_xo.astype(mm))

    x3 = pl.pallas_call(
        _mlp_kernel,
        grid=(T // MLP_ROW_TILE, d_ff // MLP_FF_TILE),
        in_specs=[pl.BlockSpec((MLP_ROW_TILE, D), lambda i, k: (i, 0)), pl.BlockSpec((1, D), lambda i, k: (0, 0)),
                  pl.BlockSpec((D, MLP_FF_TILE), lambda i, k: (0, k)),
                  pl.BlockSpec((MLP_FF_TILE, D), lambda i, k: (k, 0))],
        out_specs=pl.BlockSpec((MLP_ROW_TILE, D), lambda i, k: (i, 0)),
        out_shape=sds((T, D), F32),
        scratch_shapes=[pltpu.VMEM((MLP_ROW_TILE, D), mm)],
        compiler_params=pltpu.CompilerParams(dimension_semantics=("parallel", "arbitrary"),
                                             vmem_limit_bytes=VMEM_LIMIT),
        name="mlp",
    )(x2, row(norm3_g), w_ff_in.astype(mm), w_ff_out.astype(mm))
    return x3.reshape(B, L, D)


def kernel(x, mem, norm1_g, w_in, dsa_cq_g, dsa_ckv_g, w_dsa_uq, w_dsa_ukv, w_idx_q, idx_k_ln_g, idx_k_ln_b, dsa_qn_g, dsa_kn_g, nsa_cmp_pe, w_nsa_cmp, nsa_qn_g, nsa_kn_g, w_out, norm2_g, mem_norm_g, w_xq, w_xk, w_xv, xq_norm_g, xk_norm_g, w_xo, norm3_g, w_ff_in, w_ff_out):
    params = (norm1_g, w_in, dsa_cq_g, dsa_ckv_g, w_dsa_uq, w_dsa_ukv, w_idx_q, idx_k_ln_g, idx_k_ln_b, dsa_qn_g,
              dsa_kn_g, nsa_cmp_pe, w_nsa_cmp, nsa_qn_g, nsa_kn_g, w_out, norm2_g, mem_norm_g, w_xq, w_xk, w_xv,
              xq_norm_g, xk_norm_g, w_xo, norm3_g, w_ff_in, w_ff_out)
    for layer in range(norm1_g.shape[0]):
        x = _layer(x, mem, *(p[layer] for p in params))
    return x
```

```python
import functools
import math

import jax
import jax.numpy as jnp
import numpy as np
from jax import lax
from jax.experimental import pallas as pl
from jax.experimental.pallas import tpu as pltpu

F32 = jnp.float32
I32 = jnp.int32
_MM_DTYPE = jnp.bfloat16

HEAD_DIM = 64
HALF_DIM = HEAD_DIM // 2
ROPE_THETA = 10000.0
EPS = 1e-6
ATTN_SCALE = HEAD_DIM ** -0.5
LOG2E = math.log2(math.e)

DSA_HEADS = 8
DSA_Q_RANK = 256
DSA_KV_RANK = 128
IDX_HEADS = 8
IDX_DIM = 64
IDX_SCALE = IDX_HEADS ** -0.5 * IDX_DIM ** -0.5
DSA_TOPK_MAX = 256

NSA_HEADS = 8
NSA_KV_HEADS = 2
NSA_GROUP = NSA_HEADS // NSA_KV_HEADS
CMP_BLOCK = 32
CMP_STRIDE = 16
SLC_BLOCK = 64
SLC_COUNT = 16
WINDOW = 512
FORCE_BONUS = 1e4

XATTN_HEADS = 4
XATTN_HEAD_DIM = 128

LANES = 128
SUBLANES = 8
INT_MIN = -(2 ** 31)
KEY_LOWEST_FINITE = INT_MIN + 0x00800000
F32_LOWEST = float(np.finfo(np.float32).min)
NEG_BIG = -1e30
VMEM_LIMIT = 56 * 1024 * 1024

ROW_TILE = 1024
MLP_ROW_TILE = 1024
MLP_FF_TILE = 2048
DSA_Q_TILE = 256
NSA_Q_TILE = 128
KEY_TILE = 256
WIN_TILE = 128
V_AUG = 80
SEARCH_STRIDE = 4


def _mm(a, b):
    return jnp.dot(a.astype(_MM_DTYPE), b.astype(_MM_DTYPE), preferred_element_type=F32)


def _mm_nt(a, b):
    return lax.dot_general(a.astype(_MM_DTYPE), b.astype(_MM_DTYPE), (((1,), (1,)), ((), ())),
                           preferred_element_type=F32)


def _split(y):
    hi = y.astype(_MM_DTYPE)
    return hi, (y - hi.astype(F32)).astype(_MM_DTYPE)


def _split_mm(y, mat):
    hi, lo = _split(y)
    return jnp.dot(hi, mat, preferred_element_type=F32) + jnp.dot(lo, mat, preferred_element_type=F32)


def _split_mm_left(mat, y):
    hi, lo = _split(y)
    return jnp.dot(mat, hi, preferred_element_type=F32) + jnp.dot(mat, lo, preferred_element_type=F32)


def _rms_rows(x, g):
    return x * lax.rsqrt(jnp.mean(x * x, axis=-1, keepdims=True) + EPS) * g


def _head_rms(x, g, seg):
    return x * lax.rsqrt(_split_mm(x * x, seg) + EPS) * g


def _tile_lanes(t, n):
    return t if n == 1 else jnp.concatenate([t] * n, axis=1)


def _rope(y, cos_t, sin_t):
    w = y.shape[-1]
    lane = lax.broadcasted_iota(I32, y.shape, 1)
    swapped = jnp.where((lane & HALF_DIM) == 0, pltpu.roll(y, w - HALF_DIM, 1), pltpu.roll(y, HALF_DIM, 1))
    n = w // LANES
    return y * _tile_lanes(cos_t, n) + swapped * _tile_lanes(sin_t, n)


def _store_heads(ref, val):
    for hh in range(val.shape[1] // HEAD_DIM):
        ref[hh] = val[:, hh * HEAD_DIM:(hh + 1) * HEAD_DIM].astype(ref.dtype)


def _store_value_tiles(ref, v_t):
    tile = ref.shape[3]
    extra = ref.shape[2] - HEAD_DIM
    if extra:
        ones_row = jnp.where(lax.broadcasted_iota(I32, (extra, tile), 0) == 0, 1.0, 0.0).astype(ref.dtype)
    for g in range(ref.shape[0]):
        for k in range(ref.shape[1]):
            ref[g, k, 0:HEAD_DIM, :] = v_t[g * HEAD_DIM:(g + 1) * HEAD_DIM, k * tile:(k + 1) * tile].astype(ref.dtype)
            if extra:
                ref[g, k, HEAD_DIM:HEAD_DIM + extra, :] = ones_row


def _tree_sum(parts):
    while len(parts) > 1:
        parts = [parts[a] + parts[a + 1] for a in range(0, len(parts) - 1, 2)] + (
            [parts[-1]] if len(parts) % 2 else [])
    return parts[0]


def _memkv_kernel(mem_ref, g_ref, wk_ref, wv_ref, kg_ref, k_ref, v_ref):
    m = _rms_rows(mem_ref[0], g_ref[...])
    k = _mm(m, wk_ref[...])
    v = _mm(m, wv_ref[...])
    ks = []
    for h in range(XATTN_HEADS):
        ks.append(_rms_rows(k[:, h * XATTN_HEAD_DIM:(h + 1) * XATTN_HEAD_DIM], kg_ref[...]))
    k_ref[0] = jnp.concatenate(ks, axis=1).astype(k_ref.dtype)
    v_ref[0] = v.astype(v_ref.dtype)


def _prologue_kernel(x_ref, g1_ref, win_ref, cqg_ref, ckvg_ref, wuq_ref, wukv_ref, widx_ref,
                     lng_ref, lnb_ref, qng_ref, kng_ref, nqg_ref, nkg_ref, seg_ref, cos_ref, sin_ref,
                     qd_ref, qi_ref, qn_ref, ki_ref, kd_ref, vdt_ref, wi_ref, kc_ref, vc_ref,
                     ksl_ref, vslt_ref, kwn_ref, vwnt_ref, gate_ref):
    cos_t = cos_ref[...]
    sin_t = sin_ref[...]
    seg = seg_ref[...]
    seg1 = seg_ref[0:LANES, 0:LANES]
    q_scale = ATTN_SCALE * LOG2E

    h = _rms_rows(x_ref[...], g1_ref[...])
    proj = _mm(h, win_ref[...])

    c_q = _rms_rows(proj[:, 0:256], cqg_ref[...])
    c_kv = _rms_rows(proj[:, 256:384], ckvg_ref[...])

    q_d = _rope(_head_rms(_mm(c_q, wuq_ref[...]), qng_ref[...], seg), cos_t, sin_t) * q_scale
    _store_heads(qd_ref, q_d)
    q_i = _rope(_mm(c_q, widx_ref[...]), cos_t, sin_t)
    _store_heads(qi_ref, q_i)

    kv = _mm(c_kv, wukv_ref[...])
    k_d = _rope(_head_rms(kv, kng_ref[...], seg1), cos_t, sin_t)
    kd_ref[...] = k_d[:, 0:HEAD_DIM].astype(kd_ref.dtype)
    _store_value_tiles(vdt_ref, kv.T[HEAD_DIM:2 * HEAD_DIM, :])

    idx = proj[:, 384:512]
    mu = _split_mm(idx, seg1)
    d = idx - mu
    var = _split_mm(d * d, seg1)
    k_i = _rope(d * lax.rsqrt(var + EPS) * lng_ref[...] + lnb_ref[...], cos_t, sin_t)
    ki_ref[...] = k_i[:, 0:IDX_DIM].astype(ki_ref.dtype)
    wi_ref[...] = (idx * IDX_SCALE).T[IDX_DIM:IDX_DIM + IDX_HEADS, :]

    q_n = _rope(_head_rms(proj[:, 512:1024], nqg_ref[...], seg), cos_t, sin_t) * q_scale
    _store_heads(qn_ref, q_n)

    kc_ref[...] = proj[:, 1024:1152]
    vc_ref[...] = proj[:, 1152:1280]
    _store_heads(ksl_ref, _rope(_head_rms(proj[:, 1280:1408], nkg_ref[1:2, :], seg1), cos_t, sin_t))
    _store_value_tiles(vslt_ref, proj[:, 1408:1536].T)
    _store_heads(kwn_ref, _rope(_head_rms(proj[:, 1536:1664], nkg_ref[2:3, :], seg1), cos_t, sin_t))
    _store_value_tiles(vwnt_ref, proj[:, 1664:1792].T)
    gate_ref[...] = jax.nn.sigmoid(proj[:, 1792:1920]).T[0:gate_ref.shape[0], :]


def _compress_kernel(xk_ref, xv_ref, pe_ref, w_ref, g_ref, seg_ref, cos_ref, sin_ref, kc_ref, vc_ref):
    n = xk_ref.shape[1]

    def cmp_map(x, which):
        a = _mm(x + pe_ref[2 * which:2 * which + 1, :], w_ref[2 * which])
        b = _mm(x + pe_ref[2 * which + 1:2 * which + 2, :], w_ref[2 * which + 1])
        return a + pltpu.roll(b, n - 1, 0)

    k_c = cmp_map(xk_ref[0], 0)
    k_c = _rope(_head_rms(k_c, g_ref[...], seg_ref[...]), cos_ref[...], sin_ref[...])
    _store_heads(kc_ref.at[0], k_c)
    vc_ref[0] = cmp_map(xv_ref[0], 1).astype(vc_ref.dtype)


def _flash_pipelined(n_super, scores_fn, v_fn, s_buf, p_buf, acc_ref):
    n_lanes = acc_ref.shape[1]

    def softmax_step(slot, m):
        m_new = jnp.maximum(m, jnp.max(_load_planes(s_buf, slot), axis=0, keepdims=True))
        _store_planes(p_buf, slot, jnp.exp2(_load_planes(s_buf, slot) - m_new))
        return m_new, jnp.exp2(m - m_new)

    def value_step(slot, jj, half, alpha):
        acc_ref[...] = alpha * acc_ref[...] + jnp.dot(v_fn(jj, half), _load_planes(p_buf, slot), preferred_element_type=F32)

    _store_planes(s_buf, 0, scores_fn(0, 0))
    p_buf[1] = jnp.zeros(p_buf.shape[1:], p_buf.dtype)
    acc_ref[...] = jnp.zeros(acc_ref.shape, F32)

    def body(jj, carry):
        m, alpha = carry
        _store_planes(s_buf, 1, scores_fn(jj, 1))
        m, alpha0 = softmax_step(0, m)
        value_step(1, jnp.maximum(jj - 1, 0), 1, alpha)
        _store_planes(s_buf, 0, scores_fn(jnp.minimum(jj + 1, n_super - 1), 0))
        m, alpha1 = softmax_step(1, m)
        value_step(0, jj, 0, alpha0)
        return m, alpha1

    init = (jnp.full((1, n_lanes), NEG_BIG, F32), jnp.ones((1, n_lanes), F32))
    _, alpha = lax.fori_loop(0, n_super, body, init)
    value_step(1, n_super - 1, 1, alpha)


def _store_planes(ref, slot, val):
    for c in range(ref.shape[1]):
        ref[slot, c] = val[:, c * LANES:(c + 1) * LANES].astype(ref.dtype)


def _load_planes(ref, slot):
    return jnp.concatenate([ref[slot, c] for c in range(ref.shape[1])], axis=1)


def _normalised(acc_ref):
    return acc_ref[0:HEAD_DIM, :] / jnp.maximum(acc_ref[HEAD_DIM:HEAD_DIM + 1, :], 1e-30)


def _softmax_block(s, bias):
    s = s + bias
    mx = jnp.max(s, axis=0, keepdims=True)
    mx = jnp.where(mx > -jnp.inf, mx, 0.0)
    p = jnp.exp2(s - mx)
    return p, jnp.maximum(jnp.sum(p, axis=0, keepdims=True), 1e-30)


def _heads_to_rows(o_t, n_heads, tq):
    stacked = jnp.concatenate([o_t[:, h * tq:(h + 1) * tq] for h in range(n_heads)], axis=0)
    return stacked.T


def _key_to_float(key):
    bits = key ^ ((key >> 31) & jnp.int32(0x7FFFFFFF))
    return lax.bitcast_convert_type(bits, F32)


def _dsa_kernel(qi_ref, qd_ref, wi_ref, ki_ref, kd_ref, vt_ref, o_ref, sc_ref, s_buf, p_buf, acc_ref, *,
                k_sel, idx_bits):
    tq = qi_ref.shape[1]
    tk = vt_ref.shape[2]
    i = pl.program_id(1)
    q0 = i * tq
    n_super = (q0 + tq + 2 * tk - 1) // (2 * tk)
    n_kb = 2 * n_super
    t_row = q0 + lax.broadcasted_iota(I32, (1, tq), 1)
    w = wi_ref[...]
    q_i = qi_ref[...].reshape(IDX_HEADS * tq, IDX_DIM)
    q_d = qd_ref[...].reshape(DSA_HEADS * tq, HEAD_DIM)

    def block_rows(jj, half):
        return pl.multiple_of((2 * jj + half) * tk, tk)

    def load_scores(r0):
        return jnp.concatenate([sc_ref[c, pl.ds(r0, tk), :] for c in range(tq // LANES)], axis=1)

    def store_scores(r0, val):
        for c in range(tq // LANES):
            sc_ref[c, pl.ds(r0, tk), :] = val[:, c * LANES:(c + 1) * LANES]

    def idx_scores(jj, half):
        return _mm_nt(ki_ref[0, pl.ds(block_rows(jj, half), tk), :], q_i)

    def idx_finish(slot, jj, half):
        s = _load_planes(s_buf, slot)
        sc = jnp.maximum(s[:, 0:tq], 0.0) * w[0:1, :]
        for hh in range(1, IDX_HEADS):
            sc = sc + jnp.maximum(s[:, hh * tq:(hh + 1) * tq], 0.0) * w[hh:hh + 1, :]
        r0 = block_rows(jj, half)
        kpos = r0 + lax.broadcasted_iota(I32, (tk, 1), 0)
        store_scores(r0, jnp.where(kpos <= t_row, sc, -jnp.inf))

    _store_planes(s_buf, 0, idx_scores(0, 0))

    def idx_body(jj, carry):
        _store_planes(s_buf, 1, idx_scores(jj, 1))
        idx_finish(0, jj, 0)
        _store_planes(s_buf, 0, idx_scores(jnp.minimum(jj + 1, n_super - 1), 0))
        idx_finish(1, jj, 1)
        return carry

    lax.fori_loop(0, n_super, idx_body, 0)

    def count(indicator):
        def body(c, cnt):
            for half in range(2):
                r0 = block_rows(c, half)
                rows = r0 + lax.broadcasted_iota(I32, (tk, 1), 0)
                f = indicator(load_scores(r0), rows)
                cnt = cnt + _tree_sum([f[r * SUBLANES:(r + 1) * SUBLANES] for r in range(tk // SUBLANES)])
            return cnt
        cnt = lax.fori_loop(0, n_super, body, jnp.zeros((SUBLANES, tq), F32))
        return jnp.sum(cnt, axis=0, keepdims=True)

    def bit_cond(state):
        return jnp.logical_and(state[0] < 32, state[1] == 0)

    def bit_pass(bi, state):
        thr, n_ge, settled = state
        cand = jnp.where(settled > 0.5, thr, thr + lax.shift_left(jnp.int32(1), 31 - bi))
        cand_f = _key_to_float(cand)
        tot = count(lambda v, rows: jnp.where(v >= cand_f, 1.0, 0.0))
        keep = tot >= k_sel
        return jnp.where(keep, cand, thr), jnp.where(keep, tot, n_ge), jnp.where(tot == k_sel, 1.0, settled)

    def bit_body(state):
        bi, _, thr, n_ge, settled = state
        thr, n_ge, settled = lax.fori_loop(bi, bi + SEARCH_STRIDE, bit_pass, (thr, n_ge, settled))
        return bi + SEARCH_STRIDE, (jnp.min(settled) > 0.5).astype(I32), thr, n_ge, settled

    _, _, thr, n_ge, _ = lax.while_loop(
        bit_cond, bit_body,
        (jnp.int32(0), jnp.int32(0), jnp.full((1, tq), INT_MIN, I32), jnp.zeros((1, tq), F32),
         jnp.zeros((1, tq), F32)))
    few = thr < jnp.int32(KEY_LOWEST_FINITE)
    thr_f = jnp.where(few, F32_LOWEST, _key_to_float(jnp.maximum(thr, jnp.int32(KEY_LOWEST_FINITE))))

    excess = jnp.where(few, 0.0, jnp.where(n_ge > k_sel, 1.0, 0.0))

    @pl.when(jnp.max(excess) > 0.5)
    def _():
        need = k_sel - count(lambda v, rows: jnp.where(v > thr_f, 1.0, 0.0))

        def jbit_body(bi, j0):
            cand = j0 + lax.shift_left(jnp.int32(1), idx_bits - 1 - bi)
            tot = count(lambda v, rows: jnp.where(rows < cand, jnp.where(v == thr_f, 1.0, 0.0), 0.0))
            return jnp.where(tot < need, cand, j0)
        j0 = lax.fori_loop(0, idx_bits, jbit_body, jnp.zeros((1, tq), I32))

        def demote(c, carry):
            r0 = pl.multiple_of(c * tk, tk)
            rows = r0 + lax.broadcasted_iota(I32, (tk, 1), 0)
            v = load_scores(r0)
            demoted = jnp.where(rows > j0, jnp.where(v == thr_f, -jnp.inf, v), v)
            store_scores(r0, jnp.where(excess > 0.5, demoted, v))
            return carry
        lax.fori_loop(0, n_kb, demote, 0)

    def scores_fn(jj, half):
        r0 = block_rows(jj, half)
        sel = jnp.where(load_scores(r0) >= thr_f, 0.0, -jnp.inf)
        return _mm_nt(kd_ref[0, pl.ds(r0, tk), :], q_d) + _tile_lanes(sel, DSA_HEADS)

    _flash_pipelined(n_super, scores_fn, lambda jj, half: vt_ref[2 * jj + half], s_buf, p_buf, acc_ref)
    o_ref[...] = _heads_to_rows(_normalised(acc_ref), DSA_HEADS, tq).astype(o_ref.dtype)


def _nsa_kernel(qn_ref, kc_ref, vct_ref, ov_ref, ksl_ref, vslt_ref, kwn_ref, vwnt_ref, gate_ref, o_ref,
                sel_ref, s_buf, p_buf, acc_ref, out_ref, *, n_sel):
    tq = qn_ref.shape[1]
    n_cmp = kc_ref.shape[2]
    n_slc = ov_ref.shape[0]
    tk = vslt_ref.shape[3]
    tkw = vwnt_ref.shape[3]
    blk_per_tile = tk // SLC_BLOCK
    r_heads = NSA_GROUP
    nq = r_heads * tq
    win_keys = WINDOW + tq
    i = pl.program_id(1)
    q0 = i * tq
    t_row = q0 + lax.broadcasted_iota(I32, (1, tq), 1)
    gates = gate_ref[...]
    q_groups = [qn_ref[g * r_heads:(g + 1) * r_heads].reshape(nq, HEAD_DIM) for g in range(NSA_KV_HEADS)]

    def gate_wide(g, branch):
        return jnp.concatenate(
            [gates[(g * r_heads + r) * 3 + branch:(g * r_heads + r) * 3 + branch + 1, :] for r in range(r_heads)],
            axis=1)

    jrow = lax.broadcasted_iota(I32, (n_slc, 1), 0)
    jrow_f = jrow.astype(F32)
    tb = jnp.right_shift(t_row, SLC_BLOCK.bit_length() - 1)
    bonus = jnp.where(jrow == 0, FORCE_BONUS,
                      jnp.where(jrow == tb, FORCE_BONUS, jnp.where(jrow == tb - 1, FORCE_BONUS, 0.0)))
    importance = []
    for g in range(NSA_KV_HEADS):
        lanes = slice(g * nq, (g + 1) * nq)

        cmp_end = lax.broadcasted_iota(I32, (n_cmp, 1), 0) * CMP_STRIDE + (CMP_BLOCK - 1)
        bias = _tile_lanes(jnp.where(cmp_end <= t_row, 0.0, -jnp.inf), r_heads)
        p, l = _softmax_block(_mm_nt(kc_ref[0, g], q_groups[g]), bias)
        p = p / l
        o_g = _mm(vct_ref[0, g], p) * gate_wide(g, 0)

        p_sum = p[:, 0:tq]
        for r in range(1, r_heads):
            p_sum = p_sum + p[:, r * tq:(r + 1) * tq]
        blk = _split_mm_left(ov_ref[...], p_sum)
        importance.append(jnp.where(jrow * SLC_BLOCK <= t_row, blk + bonus, -jnp.inf))

        start = pl.multiple_of(jnp.maximum(q0 - WINDOW, 0), tkw)
        kpos = start + lax.broadcasted_iota(I32, (win_keys, 1), 0)
        bias = jnp.where(kpos <= t_row, jnp.where(kpos > t_row - WINDOW, 0.0, -jnp.inf), -jnp.inf)
        p, l = _softmax_block(_mm_nt(kwn_ref[g, pl.ds(start, win_keys), :], q_groups[g]),
                              _tile_lanes(bias, r_heads))
        o_w = None
        for k in range(win_keys // tkw):
            part = _mm(vwnt_ref[g, start // tkw + k], p[k * tkw:(k + 1) * tkw, :])
            o_w = part if o_w is None else o_w + part
        out_ref[:, lanes] = o_g + o_w / l * gate_wide(g, 2)

    val = jnp.concatenate(importance, axis=1)
    sel_bias = jnp.full(val.shape, -jnp.inf, F32)
    for _ in range(n_sel):
        top = jnp.max(val, axis=0, keepdims=True)
        first = jnp.min(jnp.where(val == top, jrow_f, float(n_slc)), axis=0, keepdims=True)
        pick = jrow_f == first
        sel_bias = jnp.where(pick, 0.0, sel_bias)
        val = jnp.where(pick, -jnp.inf, val)
    for g in range(NSA_KV_HEADS):
        sel_ref[g] = sel_bias[:, g * tq:(g + 1) * tq]

    n_super = (q0 + tq + 2 * tk - 1) // (2 * tk)
    for g in range(NSA_KV_HEADS):
        lanes = slice(g * nq, (g + 1) * nq)

        def scores_fn(jj, half, g=g):
            r0 = pl.multiple_of((2 * jj + half) * tk, tk)
            picked = sel_ref[g, pl.ds(pl.multiple_of(jj * 2 * blk_per_tile, 2 * blk_per_tile), 2 * blk_per_tile), :]
            blocks = [jnp.broadcast_to(picked[half * blk_per_tile + bb:half * blk_per_tile + bb + 1, :], (SLC_BLOCK, tq))
                      for bb in range(blk_per_tile)]
            kpos = r0 + lax.broadcasted_iota(I32, (tk, 1), 0)
            bias = jnp.where(kpos <= t_row, jnp.concatenate(blocks, axis=0), -jnp.inf)
            return _mm_nt(ksl_ref[g, pl.ds(r0, tk), :], q_groups[g]) + _tile_lanes(bias, r_heads)

        _flash_pipelined(n_super, scores_fn, lambda jj, half, g=g: vslt_ref[g, 2 * jj + half], s_buf, p_buf, acc_ref)
        out_ref[:, lanes] = out_ref[:, lanes] + _normalised(acc_ref) * gate_wide(g, 1)

    o_ref[...] = _heads_to_rows(out_ref[...], NSA_HEADS, tq).astype(o_ref.dtype)


def _post_kernel(x_ref, od_ref, on_ref, wo_ref, g2_ref, wq_ref, qg_ref, km_ref, vm_ref, wxo_ref, o_ref):
    half = od_ref.shape[1]
    x1 = x_ref[...] + _mm(od_ref[...], wo_ref[0:half, :]) + _mm(on_ref[...], wo_ref[half:2 * half, :])
    q = _mm(_rms_rows(x1, g2_ref[...]), wq_ref[...])
    km = km_ref[0]
    vm = vm_ref[0]
    heads = []
    for h in range(XATTN_HEADS):
        lo, hi = h * XATTN_HEAD_DIM, (h + 1) * XATTN_HEAD_DIM
        q_h = _rms_rows(q[:, lo:hi], qg_ref[...])
        s = _mm_nt(q_h, km[:, lo:hi]) * (XATTN_HEAD_DIM ** -0.5)
        p = jnp.exp(s - jnp.max(s, axis=-1, keepdims=True))
        heads.append(_mm(p, vm[:, lo:hi]) / jnp.sum(p, axis=-1, keepdims=True))
    o_ref[...] = x1 + _mm(jnp.concatenate(heads, axis=1), wxo_ref[...])


def _mlp_kernel(x_ref, g3_ref, w1_ref, w2_ref, o_ref, h_ref):
    @pl.when(pl.program_id(1) == 0)
    def _():
        x = x_ref[...]
        h_ref[...] = _rms_rows(x, g3_ref[...]).astype(h_ref.dtype)
        o_ref[...] = x

    a = jnp.maximum(jnp.dot(h_ref[...], w1_ref[...], preferred_element_type=F32), 0.0)
    o_ref[...] += _mm(a * a, w2_ref[...])


def _params(n_axes):
    return pltpu.CompilerParams(dimension_semantics=("parallel",) * n_axes, vmem_limit_bytes=VMEM_LIMIT)


def _full(shape):
    return pl.BlockSpec(shape, lambda *_: (0,) * len(shape))


def _rope_tables(pos):
    inv = ROPE_THETA ** (-jnp.arange(HALF_DIM, dtype=F32) / HALF_DIM)
    ang = pos.astype(F32)[:, None] * inv[None, :]
    cos, sin = jnp.cos(ang), jnp.sin(ang)
    reps = LANES // HEAD_DIM
    return (jnp.tile(jnp.concatenate([cos, cos], axis=1), (1, reps)),
            jnp.tile(jnp.concatenate([-sin, sin], axis=1), (1, reps)))


def _tile_gain(g, width):
    return jnp.tile(g.astype(F32), width // g.shape[0])[None, :]


def _layer(x, mem, norm1_g, w_in, dsa_cq_g, dsa_ckv_g, w_dsa_uq, w_dsa_ukv, w_idx_q, idx_k_ln_g, idx_k_ln_b,
           dsa_qn_g, dsa_kn_g, nsa_cmp_pe, w_nsa_cmp, nsa_qn_g, nsa_kn_g, w_out, norm2_g, mem_norm_g,
           w_xq, w_xk, w_xv, xq_norm_g, xk_norm_g, w_xo, norm3_g, w_ff_in, w_ff_out):
    B, L, D = x.shape
    M = mem.shape[1]
    mm = _MM_DTYPE
    T = B * L
    assert L % (2 * KEY_TILE) == 0 and ROW_TILE % KEY_TILE == 0 and L % ROW_TILE == 0 and T % MLP_ROW_TILE == 0 and L >= WINDOW + NSA_Q_TILE
    n_slc = L // SLC_BLOCK
    n_cmp_pad = L // CMP_STRIDE
    d_ff = w_ff_in.shape[1]
    xd = XATTN_HEADS * XATTN_HEAD_DIM
    G = NSA_KV_HEADS

    z = lambda n: jnp.zeros((D, n), F32)
    w_in_p = jnp.concatenate([
        w_in[:, 0:448], w_in[:, 448:456], z(56),
        w_in[:, 456:968],
        w_in[:, 968:1736],
        w_in[:, 1736:1760], z(104)], axis=1).astype(mm)
    seg_np = np.kron(np.eye(512 // HEAD_DIM, dtype=np.float32), np.full((HEAD_DIM, HEAD_DIM), 1.0 / HEAD_DIM, np.float32))
    seg = jnp.asarray(seg_np).astype(mm)
    cos_t, sin_t = _rope_tables(jnp.arange(L, dtype=jnp.int32))
    cos_c, sin_c = _rope_tables(jnp.arange(n_cmp_pad, dtype=jnp.int32) * CMP_STRIDE + (CMP_BLOCK - 1))
    pad64 = jnp.zeros((HEAD_DIM,), F32)
    ln_g = jnp.concatenate([idx_k_ln_g, pad64])[None, :]
    ln_b = jnp.concatenate([idx_k_ln_b, pad64])[None, :]
    nkg = jnp.tile(nsa_kn_g, (1, LANES // HEAD_DIM))
    row = lambda v: v.astype(F32)[None, :]
    sds = jax.ShapeDtypeStruct

    k_mem, v_mem = pl.pallas_call(
        _memkv_kernel,
        grid=(B,),
        in_specs=[pl.BlockSpec((1, M, D), lambda b: (b, 0, 0)), _full((1, D)), _full((D, xd)), _full((D, xd)),
                  _full((1, XATTN_HEAD_DIM))],
        out_specs=[pl.BlockSpec((1, M, xd), lambda b: (b, 0, 0))] * 2,
        out_shape=[sds((B, M, xd), mm)] * 2,
        compiler_params=_params(1), name="memkv",
    )(mem, row(mem_norm_g), w_xk.astype(mm), w_xv.astype(mm), row(xk_norm_g))

    n_rt = T // ROW_TILE
    rt_per_seq = L // ROW_TILE
    rows = lambda w: pl.BlockSpec((ROW_TILE, w), lambda i: (i, 0))
    heads = lambda n: pl.BlockSpec((n, ROW_TILE, HEAD_DIM), lambda i: (0, i, 0))
    table = pl.BlockSpec((ROW_TILE, LANES), lambda i: (i % rt_per_seq, 0))
    head_shape = lambda n: sds((n, T, HEAD_DIM), mm)
    win_per_row_tile = ROW_TILE // WIN_TILE
    keys_per_row_tile = ROW_TILE // KEY_TILE
    (qd, qi, qn, ki, kd, vd_t, wi_t, kc_raw, vc_raw, ksl, vsl_t, kwn, vwn_t, gate_t) = pl.pallas_call(
        _prologue_kernel,
        grid=(n_rt,),
        in_specs=[rows(D), _full((1, D)), _full(w_in_p.shape), _full((1, DSA_Q_RANK)), _full((1, DSA_KV_RANK)),
                  _full(w_dsa_uq.shape), _full(w_dsa_ukv.shape), _full(w_idx_q.shape),
                  _full((1, LANES)), _full((1, LANES)), _full((1, 512)), _full((1, LANES)), _full((1, 512)),
                  _full((3, LANES)), _full((512, 512)), table, table],
        out_specs=[heads(DSA_HEADS), heads(IDX_HEADS), heads(NSA_HEADS), rows(HEAD_DIM), rows(HEAD_DIM),
                   pl.BlockSpec((1, keys_per_row_tile, V_AUG, KEY_TILE), lambda i: (0, i, 0, 0)),
                   pl.BlockSpec((IDX_HEADS, ROW_TILE), lambda i: (0, i)), rows(LANES), rows(LANES),
                   heads(G), pl.BlockSpec((G, keys_per_row_tile, V_AUG, KEY_TILE), lambda i: (0, i, 0, 0)),
                   heads(G), pl.BlockSpec((G, win_per_row_tile, HEAD_DIM, WIN_TILE), lambda i: (0, i, 0, 0)),
                   pl.BlockSpec((NSA_HEADS * 3, ROW_TILE), lambda i: (0, i))],
        out_shape=[head_shape(DSA_HEADS), head_shape(IDX_HEADS), head_shape(NSA_HEADS), sds((T, HEAD_DIM), mm),
                   sds((T, HEAD_DIM), mm), sds((1, T // KEY_TILE, V_AUG, KEY_TILE), mm), sds((IDX_HEADS, T), F32),
                   sds((T, LANES), F32), sds((T, LANES), F32),
                   head_shape(G), sds((G, T // KEY_TILE, V_AUG, KEY_TILE), mm),
                   head_shape(G), sds((G, T // WIN_TILE, HEAD_DIM, WIN_TILE), mm),
                   sds((NSA_HEADS * 3, T), F32)],
        compiler_params=_params(1), name="prologue",
    )(x.reshape(T, D), row(norm1_g), w_in_p, row(dsa_cq_g), row(dsa_ckv_g), w_dsa_uq.astype(mm),
      w_dsa_ukv.astype(mm), w_idx_q.astype(mm), ln_g, ln_b, _tile_gain(dsa_qn_g, 512),
      _tile_gain(dsa_kn_g, LANES), _tile_gain(nsa_qn_g, 512), nkg, seg, cos_t, sin_t)

    half = CMP_BLOCK // 2
    eye_g = jnp.eye(G, dtype=F32)

    def cmp_weight(w):
        return jnp.einsum('lde,gh->lgdhe', w, eye_g).reshape(half * G * HEAD_DIM, G * HEAD_DIM)

    def cmp_pe(p):
        return jnp.broadcast_to(p[:, None, :], (half, G, HEAD_DIM)).reshape(1, -1)

    w_cmp = jnp.stack([cmp_weight(w_nsa_cmp[0, :half]), cmp_weight(w_nsa_cmp[0, half:]),
                       cmp_weight(w_nsa_cmp[1, :half]), cmp_weight(w_nsa_cmp[1, half:])]).astype(mm)
    pe_cmp = jnp.concatenate([cmp_pe(nsa_cmp_pe[0, :half]), cmp_pe(nsa_cmp_pe[0, half:]),
                              cmp_pe(nsa_cmp_pe[1, :half]), cmp_pe(nsa_cmp_pe[1, half:])], axis=0)
    cw = half * LANES
    k_c, v_c = pl.pallas_call(
        _compress_kernel,
        grid=(B,),
        in_specs=[pl.BlockSpec((1, n_cmp_pad, cw), lambda b: (b, 0, 0))] * 2
        + [_full((4, cw)), _full((4, cw, LANES)), _full((1, LANES)), _full((LANES, LANES)),
           _full((n_cmp_pad, LANES)), _full((n_cmp_pad, LANES))],
        out_specs=[pl.BlockSpec((1, G, n_cmp_pad, HEAD_DIM), lambda b: (b, 0, 0, 0)),
                   pl.BlockSpec((1, n_cmp_pad, LANES), lambda b: (b, 0, 0))],
        out_shape=[sds((B, G, n_cmp_pad, HEAD_DIM), mm), sds((B, n_cmp_pad, LANES), mm)],
        compiler_params=_params(1), name="compress",
    )(kc_raw.reshape(B, n_cmp_pad, cw), vc_raw.reshape(B, n_cmp_pad, cw), pe_cmp, w_cmp, nkg[0:1], seg[:LANES, :LANES],
      cos_c, sin_c)
    vc_t = v_c.reshape(B, n_cmp_pad, G, HEAD_DIM).transpose(0, 2, 3, 1)

    k_sel = min(DSA_TOPK_MAX, L // 4)
    per_seq = lambda shape: pl.BlockSpec((1,) + shape, lambda b, i: (b,) + (0,) * len(shape))
    seq_heads = pl.BlockSpec((G, L, HEAD_DIM), lambda b, i: (0, b, 0))

    def q_specs(tq):
        n_q = L // tq
        return (pl.BlockSpec((DSA_HEADS, tq, HEAD_DIM), lambda b, i: (0, b * n_q + i, 0)),
                lambda r: pl.BlockSpec((r, tq), lambda b, i: (0, b * n_q + i)),
                pl.BlockSpec((tq, DSA_HEADS * HEAD_DIM), lambda b, i: (b * n_q + i, 0)))

    tq = DSA_Q_TILE
    q_heads, q_cols, out_rows = q_specs(tq)
    o_dsa = pl.pallas_call(
        functools.partial(_dsa_kernel, k_sel=k_sel, idx_bits=int(L - 1).bit_length()),
        grid=(B, L // tq),
        in_specs=[q_heads, q_heads, q_cols(IDX_HEADS),
                  per_seq((L, IDX_DIM)), per_seq((L, HEAD_DIM)),
                  pl.BlockSpec((L // KEY_TILE, V_AUG, KEY_TILE), lambda b, i: (b, 0, 0))],
        out_specs=out_rows,
        out_shape=sds((T, DSA_HEADS * HEAD_DIM), mm),
        scratch_shapes=[pltpu.VMEM((tq // LANES, L, LANES), F32), pltpu.VMEM((2, DSA_HEADS * tq // LANES, KEY_TILE, LANES), F32),
                        pltpu.VMEM((2, DSA_HEADS * tq // LANES, KEY_TILE, LANES), mm),
                        pltpu.VMEM((V_AUG, DSA_HEADS * tq), F32)],
        compiler_params=_params(2), name="dsa",
    )(qi, qd, wi_t, ki.reshape(B, L, IDX_DIM), kd.reshape(B, L, HEAD_DIM), vd_t[0])

    ci = np.arange(n_cmp_pad)[None, :] * CMP_STRIDE
    sj = np.arange(n_slc)[:, None] * SLC_BLOCK
    ov_np = ((ci < sj + SLC_BLOCK) & (ci + CMP_BLOCK > sj) & (np.arange(n_cmp_pad)[None, :] < n_cmp_pad - 1))
    ov_t = jnp.asarray(ov_np.astype(np.float32)).astype(mm)
    tq = NSA_Q_TILE
    q_heads, q_cols, out_rows = q_specs(tq)
    o_nsa = pl.pallas_call(
        functools.partial(_nsa_kernel, n_sel=min(SLC_COUNT, n_slc)),
        grid=(B, L // tq),
        in_specs=[q_heads, per_seq((G, n_cmp_pad, HEAD_DIM)), per_seq((G, HEAD_DIM, n_cmp_pad)),
                  pl.BlockSpec((n_slc, n_cmp_pad), lambda b, i: (0, 0)),
                  seq_heads, pl.BlockSpec((G, L // KEY_TILE, V_AUG, KEY_TILE), lambda b, i: (0, b, 0, 0)),
                  seq_heads, pl.BlockSpec((G, L // WIN_TILE, HEAD_DIM, WIN_TILE), lambda b, i: (0, b, 0, 0)),
                  q_cols(NSA_HEADS * 3)],
        out_specs=out_rows,
        out_shape=sds((T, NSA_HEADS * HEAD_DIM), mm),
        scratch_shapes=[pltpu.VMEM((G, n_slc, tq), F32), pltpu.VMEM((2, NSA_GROUP * tq // LANES, KEY_TILE, LANES), F32),
                        pltpu.VMEM((2, NSA_GROUP * tq // LANES, KEY_TILE, LANES), mm),
                        pltpu.VMEM((V_AUG, NSA_GROUP * tq), F32),
                        pltpu.VMEM((HEAD_DIM, NSA_HEADS * tq), F32)],
        compiler_params=_params(2), name="nsa",
    )(qn, k_c, vc_t, ov_t, ksl, vsl_t, kwn, vwn_t, gate_t)

    x2 = pl.pallas_call(
        _post_kernel,
        grid=(n_rt,),
        in_specs=[rows(D), rows(512), rows(512), _full((D, D)), _full((1, D)), _full((D, xd)),
                  _full((1, XATTN_HEAD_DIM)),
                  pl.BlockSpec((1, M, xd), lambda i: (i // rt_per_seq, 0, 0)),
                  pl.BlockSpec((1, M, xd), lambda i: (i // rt_per_seq, 0, 0)), _full((xd, D))],
        out_specs=rows(D),
        out_shape=sds((T, D), F32),
        compiler_params=_params(1), name="post",
    )(x.reshape(T, D), o_dsa, o_nsa, w_out.astype(mm), row(norm2_g), w_xq.astype(mm), row(xq_norm_g), k_mem, v_mem,
      w_xo.astype(mm))

    x3 = pl.pallas_call(
        _mlp_kernel,
        grid=(T // MLP_ROW_TILE, d_ff // MLP_FF_TILE),
        in_specs=[pl.BlockSpec((MLP_ROW_TILE, D), lambda i, k: (i, 0)), pl.BlockSpec((1, D), lambda i, k: (0, 0)),
                  pl.BlockSpec((D, MLP_FF_TILE), lambda i, k: (0, k)),
                  pl.BlockSpec((MLP_FF_TILE, D), lambda i, k: (k, 0))],
        out_specs=pl.BlockSpec((MLP_ROW_TILE, D), lambda i, k: (i, 0)),
        out_shape=sds((T, D), F32),
        scratch_shapes=[pltpu.VMEM((MLP_ROW_TILE, D), mm)],
        compiler_params=pltpu.CompilerParams(dimension_semantics=("parallel", "arbitrary"),
                                             vmem_limit_bytes=VMEM_LIMIT),
        name="mlp",
    )(x2, row(norm3_g), w_ff_in.astype(mm), w_ff_out.astype(mm))
    return x3.reshape(B, L, D)


def kernel(x, mem, norm1_g, w_in, dsa_cq_g, dsa_ckv_g, w_dsa_uq, w_dsa_ukv, w_idx_q, idx_k_ln_g, idx_k_ln_b, dsa_qn_g, dsa_kn_g, nsa_cmp_pe, w_nsa_cmp, nsa_qn_g, nsa_kn_g, w_out, norm2_g, mem_norm_g, w_xq, w_xk, w_xv, xq_norm_g, xk_norm_g, w_xo, norm3_g, w_ff_in, w_ff_out):
    params = (norm1_g, w_in, dsa_cq_g, dsa_ckv_g, w_dsa_uq, w_dsa_ukv, w_idx_q, idx_k_ln_g, idx_k_ln_b, dsa_qn_g,
              dsa_kn_g, nsa_cmp_pe, w_nsa_cmp, nsa_qn_g, nsa_kn_g, w_out, norm2_g, mem_norm_g, w_xq, w_xk, w_xv,
              xq_norm_g, xk_norm_g, w_xo, norm3_g, w_ff_in, w_ff_out)
    for layer in range(norm1_g.shape[0]):
        x = _layer(x, mem, *(p[layer] for p in params))
    return x
```

```python
import functools
import math

import jax
import jax.numpy as jnp
import numpy as np
from jax import lax
from jax.experimental import pallas as pl
from jax.experimental.pallas import tpu as pltpu

F32 = jnp.float32
I32 = jnp.int32
_MM_DTYPE = jnp.bfloat16

HEAD_DIM = 64
HALF_DIM = HEAD_DIM // 2
ROPE_THETA = 10000.0
EPS = 1e-6
ATTN_SCALE = HEAD_DIM ** -0.5
LOG2E = math.log2(math.e)

DSA_HEADS = 8
DSA_Q_RANK = 256
DSA_KV_RANK = 128
IDX_HEADS = 8
IDX_DIM = 64
IDX_SCALE = IDX_HEADS ** -0.5 * IDX_DIM ** -0.5
DSA_TOPK_MAX = 256

NSA_HEADS = 8
NSA_KV_HEADS = 2
NSA_GROUP = NSA_HEADS // NSA_KV_HEADS
CMP_BLOCK = 32
CMP_STRIDE = 16
SLC_BLOCK = 64
SLC_COUNT = 16
WINDOW = 512
FORCE_BONUS = 1e4

XATTN_HEADS = 4
XATTN_HEAD_DIM = 128

LANES = 128
SUBLANES = 8
INT_MIN = -(2 ** 31)
KEY_LOWEST_FINITE = INT_MIN + 0x00800000
F32_LOWEST = float(np.finfo(np.float32).min)
NEG_BIG = -1e30
VMEM_LIMIT = 56 * 1024 * 1024

ROW_TILE = 1024
MLP_ROW_TILE = 512
MLP_FF_TILE = 4096
DSA_Q_TILE = 256
NSA_Q_TILE = 128
KEY_TILE = 256
WIN_TILE = 128
V_AUG = 80
SEARCH_STRIDE = 4


def _mm(a, b):
    return jnp.dot(a.astype(_MM_DTYPE), b.astype(_MM_DTYPE), preferred_element_type=F32)


def _mm_nt(a, b):
    return lax.dot_general(a.astype(_MM_DTYPE), b.astype(_MM_DTYPE), (((1,), (1,)), ((), ())),
                           preferred_element_type=F32)


def _split(y):
    hi = y.astype(_MM_DTYPE)
    return hi, (y - hi.astype(F32)).astype(_MM_DTYPE)


def _split_mm(y, mat):
    hi, lo = _split(y)
    return jnp.dot(hi, mat, preferred_element_type=F32) + jnp.dot(lo, mat, preferred_element_type=F32)


def _split_mm_left(mat, y):
    hi, lo = _split(y)
    return jnp.dot(mat, hi, preferred_element_type=F32) + jnp.dot(mat, lo, preferred_element_type=F32)


def _rms_rows(x, g):
    return x * lax.rsqrt(jnp.mean(x * x, axis=-1, keepdims=True) + EPS) * g


def _head_rms(x, g, seg):
    return x * lax.rsqrt(_split_mm(x * x, seg) + EPS) * g


def _tile_lanes(t, n):
    return t if n == 1 else jnp.concatenate([t] * n, axis=1)


def _rope(y, cos_t, sin_t):
    w = y.shape[-1]
    lane = lax.broadcasted_iota(I32, y.shape, 1)
    swapped = jnp.where((lane & HALF_DIM) == 0, pltpu.roll(y, w - HALF_DIM, 1), pltpu.roll(y, HALF_DIM, 1))
    n = w // LANES
    return y * _tile_lanes(cos_t, n) + swapped * _tile_lanes(sin_t, n)


def _store_heads(ref, val):
    for hh in range(val.shape[1] // HEAD_DIM):
        ref[hh] = val[:, hh * HEAD_DIM:(hh + 1) * HEAD_DIM].astype(ref.dtype)


def _store_value_tiles(ref, v_t):
    tile = ref.shape[3]
    extra = ref.shape[2] - HEAD_DIM
    if extra:
        ones_row = jnp.where(lax.broadcasted_iota(I32, (extra, tile), 0) == 0, 1.0, 0.0).astype(ref.dtype)
    for g in range(ref.shape[0]):
        for k in range(ref.shape[1]):
            ref[g, k, 0:HEAD_DIM, :] = v_t[g * HEAD_DIM:(g + 1) * HEAD_DIM, k * tile:(k + 1) * tile].astype(ref.dtype)
            if extra:
                ref[g, k, HEAD_DIM:HEAD_DIM + extra, :] = ones_row


def _tree_sum(parts):
    while len(parts) > 1:
        parts = [parts[a] + parts[a + 1] for a in range(0, len(parts) - 1, 2)] + (
            [parts[-1]] if len(parts) % 2 else [])
    return parts[0]


def _memkv_kernel(mem_ref, g_ref, wk_ref, wv_ref, kg_ref, k_ref, v_ref):
    m = _rms_rows(mem_ref[0], g_ref[...])
    k = _mm(m, wk_ref[...])
    v = _mm(m, wv_ref[...])
    ks = []
    for h in range(XATTN_HEADS):
        ks.append(_rms_rows(k[:, h * XATTN_HEAD_DIM:(h + 1) * XATTN_HEAD_DIM], kg_ref[...]))
    k_ref[0] = jnp.concatenate(ks, axis=1).astype(k_ref.dtype)
    v_ref[0] = v.astype(v_ref.dtype)


def _prologue_kernel(x_ref, g1_ref, win_ref, cqg_ref, ckvg_ref, wuq_ref, wukv_ref, widx_ref,
                     lng_ref, lnb_ref, qng_ref, kng_ref, nqg_ref, nkg_ref, seg_ref, cos_ref, sin_ref,
                     qd_ref, qi_ref, qn_ref, ki_ref, kd_ref, vdt_ref, wi_ref, kc_ref, vc_ref,
                     ksl_ref, vslt_ref, kwn_ref, vwnt_ref, gate_ref):
    cos_t = cos_ref[...]
    sin_t = sin_ref[...]
    seg = seg_ref[...]
    seg1 = seg_ref[0:LANES, 0:LANES]
    q_scale = ATTN_SCALE * LOG2E

    h = _rms_rows(x_ref[...], g1_ref[...])
    proj = _mm(h, win_ref[...])

    c_q = _rms_rows(proj[:, 0:256], cqg_ref[...])
    c_kv = _rms_rows(proj[:, 256:384], ckvg_ref[...])

    q_d = _rope(_head_rms(_mm(c_q, wuq_ref[...]), qng_ref[...], seg), cos_t, sin_t) * q_scale
    _store_heads(qd_ref, q_d)
    q_i = _rope(_mm(c_q, widx_ref[...]), cos_t, sin_t)
    _store_heads(qi_ref, q_i)

    kv = _mm(c_kv, wukv_ref[...])
    k_d = _rope(_head_rms(kv, kng_ref[...], seg1), cos_t, sin_t)
    kd_ref[...] = k_d[:, 0:HEAD_DIM].astype(kd_ref.dtype)
    _store_value_tiles(vdt_ref, kv.T[HEAD_DIM:2 * HEAD_DIM, :])

    idx = proj[:, 384:512]
    mu = _split_mm(idx, seg1)
    d = idx - mu
    var = _split_mm(d * d, seg1)
    k_i = _rope(d * lax.rsqrt(var + EPS) * lng_ref[...] + lnb_ref[...], cos_t, sin_t)
    ki_ref[...] = k_i[:, 0:IDX_DIM].astype(ki_ref.dtype)
    wi_ref[...] = (idx * IDX_SCALE).T[IDX_DIM:IDX_DIM + IDX_HEADS, :]

    q_n = _rope(_head_rms(proj[:, 512:1024], nqg_ref[...], seg), cos_t, sin_t) * q_scale
    _store_heads(qn_ref, q_n)

    kc_ref[...] = proj[:, 1024:1152]
    vc_ref[...] = proj[:, 1152:1280]
    _store_heads(ksl_ref, _rope(_head_rms(proj[:, 1280:1408], nkg_ref[1:2, :], seg1), cos_t, sin_t))
    _store_value_tiles(vslt_ref, proj[:, 1408:1536].T)
    _store_heads(kwn_ref, _rope(_head_rms(proj[:, 1536:1664], nkg_ref[2:3, :], seg1), cos_t, sin_t))
    _store_value_tiles(vwnt_ref, proj[:, 1664:1792].T)
    gate_ref[...] = jax.nn.sigmoid(proj[:, 1792:1920]).T[0:gate_ref.shape[0], :]


def _compress_kernel(xk_ref, xv_ref, pe_ref, w_ref, g_ref, seg_ref, cos_ref, sin_ref, kc_ref, vc_ref):
    n = xk_ref.shape[1]

    def cmp_map(x, which):
        a = _mm(x + pe_ref[2 * which:2 * which + 1, :], w_ref[2 * which])
        b = _mm(x + pe_ref[2 * which + 1:2 * which + 2, :], w_ref[2 * which + 1])
        return a + pltpu.roll(b, n - 1, 0)

    k_c = cmp_map(xk_ref[0], 0)
    k_c = _rope(_head_rms(k_c, g_ref[...], seg_ref[...]), cos_ref[...], sin_ref[...])
    _store_heads(kc_ref.at[0], k_c)
    vc_ref[0] = cmp_map(xv_ref[0], 1).astype(vc_ref.dtype)


def _flash_pipelined(n_super, scores_fn, v_fn, s_buf, p_buf, acc_ref):
    n_lanes = acc_ref.shape[1]

    def softmax_step(slot, m):
        m_new = jnp.maximum(m, jnp.max(_load_planes(s_buf, slot), axis=0, keepdims=True))
        _store_planes(p_buf, slot, jnp.exp2(_load_planes(s_buf, slot) - m_new))
        return m_new, jnp.exp2(m - m_new)

    def value_step(slot, jj, half, alpha):
        acc_ref[...] = alpha * acc_ref[...] + jnp.dot(v_fn(jj, half), _load_planes(p_buf, slot), preferred_element_type=F32)

    _store_planes(s_buf, 0, scores_fn(0, 0))
    p_buf[1] = jnp.zeros(p_buf.shape[1:], p_buf.dtype)
    acc_ref[...] = jnp.zeros(acc_ref.shape, F32)

    def body(jj, carry, last=False):
        m, alpha = carry
        _store_planes(s_buf, 1, scores_fn(jj, 1))
        m, alpha0 = softmax_step(0, m)
        value_step(1, jnp.maximum(jj - 1, 0), 1, alpha)
        if not last:
            _store_planes(s_buf, 0, scores_fn(jj + 1, 0))
        m, alpha1 = softmax_step(1, m)
        value_step(0, jj, 0, alpha0)
        return m, alpha1

    init = (jnp.full((1, n_lanes), NEG_BIG, F32), jnp.ones((1, n_lanes), F32))
    carry = lax.fori_loop(0, n_super - 1, body, init)
    _, alpha = body(n_super - 1, carry, last=True)
    value_step(1, n_super - 1, 1, alpha)


def _store_planes(ref, slot, val):
    for c in range(ref.shape[1]):
        ref[slot, c] = val[:, c * LANES:(c + 1) * LANES].astype(ref.dtype)


def _load_planes(ref, slot):
    return jnp.concatenate([ref[slot, c] for c in range(ref.shape[1])], axis=1)


def _normalised(acc_ref):
    return acc_ref[0:HEAD_DIM, :] / jnp.maximum(acc_ref[HEAD_DIM:HEAD_DIM + 1, :], 1e-30)


def _softmax_block(s, bias):
    s = s + bias
    mx = jnp.max(s, axis=0, keepdims=True)
    mx = jnp.where(mx > -jnp.inf, mx, 0.0)
    p = jnp.exp2(s - mx)
    return p, jnp.maximum(jnp.sum(p, axis=0, keepdims=True), 1e-30)


def _heads_to_rows(o_t, n_heads, tq):
    stacked = jnp.concatenate([o_t[:, h * tq:(h + 1) * tq] for h in range(n_heads)], axis=0)
    return stacked.T


def _key_to_float(key):
    bits = key ^ ((key >> 31) & jnp.int32(0x7FFFFFFF))
    return lax.bitcast_convert_type(bits, F32)


def _dsa_kernel(qi_ref, qd_ref, wi_ref, ki_ref, kd_ref, vt_ref, o_ref, sc_ref, s_buf, p_buf, acc_ref, *,
                k_sel, idx_bits):
    tq = qi_ref.shape[1]
    tk = vt_ref.shape[2]
    i = pl.program_id(1)
    q0 = i * tq
    n_super = (q0 + tq + 2 * tk - 1) // (2 * tk)
    n_kb = 2 * n_super
    t_row = q0 + lax.broadcasted_iota(I32, (1, tq), 1)
    w = wi_ref[...]
    q_i = qi_ref[...].reshape(IDX_HEADS * tq, IDX_DIM)
    q_d = qd_ref[...].reshape(DSA_HEADS * tq, HEAD_DIM)

    def block_rows(jj, half):
        return pl.multiple_of((2 * jj + half) * tk, tk)

    def load_scores(r0):
        return jnp.concatenate([sc_ref[c, pl.ds(r0, tk), :] for c in range(tq // LANES)], axis=1)

    def store_scores(r0, val):
        for c in range(tq // LANES):
            sc_ref[c, pl.ds(r0, tk), :] = val[:, c * LANES:(c + 1) * LANES]

    def idx_scores(jj, half):
        return _mm_nt(ki_ref[0, pl.ds(block_rows(jj, half), tk), :], q_i)

    def idx_finish(slot, jj, half):
        s = _load_planes(s_buf, slot)
        sc = jnp.maximum(s[:, 0:tq], 0.0) * w[0:1, :]
        for hh in range(1, IDX_HEADS):
            sc = sc + jnp.maximum(s[:, hh * tq:(hh + 1) * tq], 0.0) * w[hh:hh + 1, :]
        r0 = block_rows(jj, half)
        kpos = r0 + lax.broadcasted_iota(I32, (tk, 1), 0)
        store_scores(r0, jnp.where(kpos <= t_row, sc, -jnp.inf))

    _store_planes(s_buf, 0, idx_scores(0, 0))

    def idx_body(jj, carry, last=False):
        _store_planes(s_buf, 1, idx_scores(jj, 1))
        idx_finish(0, jj, 0)
        if not last:
            _store_planes(s_buf, 0, idx_scores(jj + 1, 0))
        idx_finish(1, jj, 1)
        return carry

    lax.fori_loop(0, n_super - 1, idx_body, 0)
    idx_body(n_super - 1, 0, last=True)

    def count(indicator):
        def body(c, cnt):
            for half in range(2):
                r0 = block_rows(c, half)
                rows = r0 + lax.broadcasted_iota(I32, (tk, 1), 0)
                f = indicator(load_scores(r0), rows)
                cnt = cnt + _tree_sum([f[r * SUBLANES:(r + 1) * SUBLANES] for r in range(tk // SUBLANES)])
            return cnt
        cnt = lax.fori_loop(0, n_super, body, jnp.zeros((SUBLANES, tq), F32))
        return jnp.sum(cnt, axis=0, keepdims=True)

    def bit_cond(state):
        return jnp.logical_and(state[0] < 32, state[1] == 0)

    def bit_pass(bi, state):
        thr, n_ge, settled = state
        cand = jnp.where(settled > 0.5, thr, thr + lax.shift_left(jnp.int32(1), 31 - bi))
        cand_f = _key_to_float(cand)
        tot = count(lambda v, rows: jnp.where(v >= cand_f, 1.0, 0.0))
        keep = tot >= k_sel
        return jnp.where(keep, cand, thr), jnp.where(keep, tot, n_ge), jnp.where(tot == k_sel, 1.0, settled)

    def bit_body(state):
        bi, _, thr, n_ge, settled = state
        thr, n_ge, settled = lax.fori_loop(bi, bi + SEARCH_STRIDE, bit_pass, (thr, n_ge, settled))
        return bi + SEARCH_STRIDE, (jnp.min(settled) > 0.5).astype(I32), thr, n_ge, settled

    _, _, thr, n_ge, _ = lax.while_loop(
        bit_cond, bit_body,
        (jnp.int32(0), jnp.int32(0), jnp.full((1, tq), INT_MIN, I32), jnp.zeros((1, tq), F32),
         jnp.zeros((1, tq), F32)))
    few = thr < jnp.int32(KEY_LOWEST_FINITE)
    thr_f = jnp.where(few, F32_LOWEST, _key_to_float(jnp.maximum(thr, jnp.int32(KEY_LOWEST_FINITE))))

    excess = jnp.where(few, 0.0, jnp.where(n_ge > k_sel, 1.0, 0.0))

    @pl.when(jnp.max(excess) > 0.5)
    def _():
        need = k_sel - count(lambda v, rows: jnp.where(v > thr_f, 1.0, 0.0))

        def jbit_body(bi, j0):
            cand = j0 + lax.shift_left(jnp.int32(1), idx_bits - 1 - bi)
            tot = count(lambda v, rows: jnp.where(rows < cand, jnp.where(v == thr_f, 1.0, 0.0), 0.0))
            return jnp.where(tot < need, cand, j0)
        j0 = lax.fori_loop(0, idx_bits, jbit_body, jnp.zeros((1, tq), I32))

        def demote(c, carry):
            r0 = pl.multiple_of(c * tk, tk)
            rows = r0 + lax.broadcasted_iota(I32, (tk, 1), 0)
            v = load_scores(r0)
            demoted = jnp.where(rows > j0, jnp.where(v == thr_f, -jnp.inf, v), v)
            store_scores(r0, jnp.where(excess > 0.5, demoted, v))
            return carry
        lax.fori_loop(0, n_kb, demote, 0)

    def scores_fn(jj, half):
        r0 = block_rows(jj, half)
        sel = jnp.where(load_scores(r0) >= thr_f, 0.0, -jnp.inf)
        return _mm_nt(kd_ref[0, pl.ds(r0, tk), :], q_d) + _tile_lanes(sel, DSA_HEADS)

    _flash_pipelined(n_super, scores_fn, lambda jj, half: vt_ref[2 * jj + half], s_buf, p_buf, acc_ref)
    o_ref[...] = _heads_to_rows(_normalised(acc_ref), DSA_HEADS, tq).astype(o_ref.dtype)


def _nsa_kernel(qn_ref, kc_ref, vct_ref, ov_ref, ksl_ref, vslt_ref, kwn_ref, vwnt_ref, gate_ref, o_ref,
                sel_ref, s_buf, p_buf, acc_ref, out_ref, *, n_sel):
    tq = qn_ref.shape[1]
    n_cmp = kc_ref.shape[2]
    n_slc = ov_ref.shape[0]
    tk = vslt_ref.shape[3]
    tkw = vwnt_ref.shape[3]
    blk_per_tile = tk // SLC_BLOCK
    r_heads = NSA_GROUP
    nq = r_heads * tq
    win_keys = WINDOW + tq
    i = pl.program_id(1)
    q0 = i * tq
    t_row = q0 + lax.broadcasted_iota(I32, (1, tq), 1)
    gates = gate_ref[...]
    q_groups = [qn_ref[g * r_heads:(g + 1) * r_heads].reshape(nq, HEAD_DIM) for g in range(NSA_KV_HEADS)]

    def gate_wide(g, branch):
        return jnp.concatenate(
            [gates[(g * r_heads + r) * 3 + branch:(g * r_heads + r) * 3 + branch + 1, :] for r in range(r_heads)],
            axis=1)

    jrow = lax.broadcasted_iota(I32, (n_slc, 1), 0)
    jrow_f = jrow.astype(F32)
    tb = jnp.right_shift(t_row, SLC_BLOCK.bit_length() - 1)
    bonus = jnp.where(jrow == 0, FORCE_BONUS,
                      jnp.where(jrow == tb, FORCE_BONUS, jnp.where(jrow == tb - 1, FORCE_BONUS, 0.0)))
    importance = []
    for g in range(NSA_KV_HEADS):
        lanes = slice(g * nq, (g + 1) * nq)

        cmp_end = lax.broadcasted_iota(I32, (n_cmp, 1), 0) * CMP_STRIDE + (CMP_BLOCK - 1)
        bias = _tile_lanes(jnp.where(cmp_end <= t_row, 0.0, -jnp.inf), r_heads)
        p, l = _softmax_block(_mm_nt(kc_ref[0, g], q_groups[g]), bias)
        p = p / l
        o_g = _mm(vct_ref[0, g], p) * gate_wide(g, 0)

        p_sum = p[:, 0:tq]
        for r in range(1, r_heads):
            p_sum = p_sum + p[:, r * tq:(r + 1) * tq]
        blk = _split_mm_left(ov_ref[...], p_sum)
        importance.append(jnp.where(jrow * SLC_BLOCK <= t_row, blk + bonus, -jnp.inf))

        start = pl.multiple_of(jnp.maximum(q0 - WINDOW, 0), tkw)
        kpos = start + lax.broadcasted_iota(I32, (win_keys, 1), 0)
        bias = jnp.where(kpos <= t_row, jnp.where(kpos > t_row - WINDOW, 0.0, -jnp.inf), -jnp.inf)
        p, l = _softmax_block(_mm_nt(kwn_ref[g, pl.ds(start, win_keys), :], q_groups[g]),
                              _tile_lanes(bias, r_heads))
        o_w = None
        for k in range(win_keys // tkw):
            part = _mm(vwnt_ref[g, start // tkw + k], p[k * tkw:(k + 1) * tkw, :])
            o_w = part if o_w is None else o_w + part
        out_ref[:, lanes] = o_g + o_w / l * gate_wide(g, 2)

    val = jnp.concatenate(importance, axis=1)
    sel_bias = jnp.full(val.shape, -jnp.inf, F32)
    for _ in range(n_sel):
        top = jnp.max(val, axis=0, keepdims=True)
        first = jnp.min(jnp.where(val == top, jrow_f, float(n_slc)), axis=0, keepdims=True)
        pick = jrow_f == first
        sel_bias = jnp.where(pick, 0.0, sel_bias)
        val = jnp.where(pick, -jnp.inf, val)
    for g in range(NSA_KV_HEADS):
        sel_ref[g] = sel_bias[:, g * tq:(g + 1) * tq]

    n_super = (q0 + tq + 2 * tk - 1) // (2 * tk)
    for g in range(NSA_KV_HEADS):
        lanes = slice(g * nq, (g + 1) * nq)

        def scores_fn(jj, half, g=g):
            r0 = pl.multiple_of((2 * jj + half) * tk, tk)
            picked = sel_ref[g, pl.ds(pl.multiple_of(jj * 2 * blk_per_tile, 2 * blk_per_tile), 2 * blk_per_tile), :]
            blocks = [jnp.broadcast_to(picked[half * blk_per_tile + bb:half * blk_per_tile + bb + 1, :], (SLC_BLOCK, tq))
                      for bb in range(blk_per_tile)]
            kpos = r0 + lax.broadcasted_iota(I32, (tk, 1), 0)
            bias = jnp.where(kpos <= t_row, jnp.concatenate(blocks, axis=0), -jnp.inf)
            return _mm_nt(ksl_ref[g, pl.ds(r0, tk), :], q_groups[g]) + _tile_lanes(bias, r_heads)

        _flash_pipelined(n_super, scores_fn, lambda jj, half, g=g: vslt_ref[g, 2 * jj + half], s_buf, p_buf, acc_ref)
        out_ref[:, lanes] = out_ref[:, lanes] + _normalised(acc_ref) * gate_wide(g, 1)

    o_ref[...] = _heads_to_rows(out_ref[...], NSA_HEADS, tq).astype(o_ref.dtype)


def _post_kernel(x_ref, od_ref, on_ref, wo_ref, g2_ref, wq_ref, qg_ref, km_ref, vm_ref, wxo_ref, o_ref):
    half = od_ref.shape[1]
    x1 = x_ref[...] + _mm(od_ref[...], wo_ref[0:half, :]) + _mm(on_ref[...], wo_ref[half:2 * half, :])
    q = _mm(_rms_rows(x1, g2_ref[...]), wq_ref[...])
    km = km_ref[0]
    vm = vm_ref[0]
    heads = []
    for h in range(XATTN_HEADS):
        lo, hi = h * XATTN_HEAD_DIM, (h + 1) * XATTN_HEAD_DIM
        q_h = _rms_rows(q[:, lo:hi], qg_ref[...])
        s = _mm_nt(q_h, km[:, lo:hi]) * (XATTN_HEAD_DIM ** -0.5)
        p = jnp.exp(s - jnp.max(s, axis=-1, keepdims=True))
        heads.append(_mm(p, vm[:, lo:hi]) / jnp.sum(p, axis=-1, keepdims=True))
    o_ref[...] = x1 + _mm(jnp.concatenate(heads, axis=1), wxo_ref[...])


def _mlp_kernel(x_ref, g3_ref, w1_ref, w2_ref, o_ref, h_ref):
    @pl.when(pl.program_id(1) == 0)
    def _():
        x = x_ref[...]
        h_ref[...] = _rms_rows(x, g3_ref[...]).astype(h_ref.dtype)
        o_ref[...] = x

    a = jnp.maximum(jnp.dot(h_ref[...], w1_ref[...], preferred_element_type=F32), 0.0)
    o_ref[...] += _mm(a * a, w2_ref[...])


def _params(n_axes):
    return pltpu.CompilerParams(dimension_semantics=("parallel",) * n_axes, vmem_limit_bytes=VMEM_LIMIT)


def _full(shape):
    return pl.BlockSpec(shape, lambda *_: (0,) * len(shape))


def _rope_tables(pos):
    inv = ROPE_THETA ** (-jnp.arange(HALF_DIM, dtype=F32) / HALF_DIM)
    ang = pos.astype(F32)[:, None] * inv[None, :]
    cos, sin = jnp.cos(ang), jnp.sin(ang)
    reps = LANES // HEAD_DIM
    return (jnp.tile(jnp.concatenate([cos, cos], axis=1), (1, reps)),
            jnp.tile(jnp.concatenate([-sin, sin], axis=1), (1, reps)))


def _tile_gain(g, width):
    return jnp.tile(g.astype(F32), width // g.shape[0])[None, :]


def _layer(x, mem, norm1_g, w_in, dsa_cq_g, dsa_ckv_g, w_dsa_uq, w_dsa_ukv, w_idx_q, idx_k_ln_g, idx_k_ln_b,
           dsa_qn_g, dsa_kn_g, nsa_cmp_pe, w_nsa_cmp, nsa_qn_g, nsa_kn_g, w_out, norm2_g, mem_norm_g,
           w_xq, w_xk, w_xv, xq_norm_g, xk_norm_g, w_xo, norm3_g, w_ff_in, w_ff_out):
    B, L, D = x.shape
    M = mem.shape[1]
    mm = _MM_DTYPE
    T = B * L
    assert L % (2 * KEY_TILE) == 0 and ROW_TILE % KEY_TILE == 0 and L % ROW_TILE == 0 and T % MLP_ROW_TILE == 0 and L >= WINDOW + NSA_Q_TILE
    n_slc = L // SLC_BLOCK
    n_cmp_pad = L // CMP_STRIDE
    d_ff = w_ff_in.shape[1]
    xd = XATTN_HEADS * XATTN_HEAD_DIM
    G = NSA_KV_HEADS

    z = lambda n: jnp.zeros((D, n), F32)
    w_in_p = jnp.concatenate([
        w_in[:, 0:448], w_in[:, 448:456], z(56),
        w_in[:, 456:968],
        w_in[:, 968:1736],
        w_in[:, 1736:1760], z(104)], axis=1).astype(mm)
    seg_np = np.kron(np.eye(512 // HEAD_DIM, dtype=np.float32), np.full((HEAD_DIM, HEAD_DIM), 1.0 / HEAD_DIM, np.float32))
    seg = jnp.asarray(seg_np).astype(mm)
    cos_t, sin_t = _rope_tables(jnp.arange(L, dtype=jnp.int32))
    cos_c, sin_c = _rope_tables(jnp.arange(n_cmp_pad, dtype=jnp.int32) * CMP_STRIDE + (CMP_BLOCK - 1))
    pad64 = jnp.zeros((HEAD_DIM,), F32)
    ln_g = jnp.concatenate([idx_k_ln_g, pad64])[None, :]
    ln_b = jnp.concatenate([idx_k_ln_b, pad64])[None, :]
    nkg = jnp.tile(nsa_kn_g, (1, LANES // HEAD_DIM))
    row = lambda v: v.astype(F32)[None, :]
    sds = jax.ShapeDtypeStruct

    k_mem, v_mem = pl.pallas_call(
        _memkv_kernel,
        grid=(B,),
        in_specs=[pl.BlockSpec((1, M, D), lambda b: (b, 0, 0)), _full((1, D)), _full((D, xd)), _full((D, xd)),
                  _full((1, XATTN_HEAD_DIM))],
        out_specs=[pl.BlockSpec((1, M, xd), lambda b: (b, 0, 0))] * 2,
        out_shape=[sds((B, M, xd), mm)] * 2,
        compiler_params=_params(1), name="memkv",
    )(mem, row(mem_norm_g), w_xk.astype(mm), w_xv.astype(mm), row(xk_norm_g))

    n_rt = T // ROW_TILE
    rt_per_seq = L // ROW_TILE
    rows = lambda w: pl.BlockSpec((ROW_TILE, w), lambda i: (i, 0))
    heads = lambda n: pl.BlockSpec((n, ROW_TILE, HEAD_DIM), lambda i: (0, i, 0))
    table = pl.BlockSpec((ROW_TILE, LANES), lambda i: (i % rt_per_seq, 0))
    head_shape = lambda n: sds((n, T, HEAD_DIM), mm)
    win_per_row_tile = ROW_TILE // WIN_TILE
    keys_per_row_tile = ROW_TILE // KEY_TILE
    (qd, qi, qn, ki, kd, vd_t, wi_t, kc_raw, vc_raw, ksl, vsl_t, kwn, vwn_t, gate_t) = pl.pallas_call(
        _prologue_kernel,
        grid=(n_rt,),
        in_specs=[rows(D), _full((1, D)), _full(w_in_p.shape), _full((1, DSA_Q_RANK)), _full((1, DSA_KV_RANK)),
                  _full(w_dsa_uq.shape), _full(w_dsa_ukv.shape), _full(w_idx_q.shape),
                  _full((1, LANES)), _full((1, LANES)), _full((1, 512)), _full((1, LANES)), _full((1, 512)),
                  _full((3, LANES)), _full((512, 512)), table, table],
        out_specs=[heads(DSA_HEADS), heads(IDX_HEADS), heads(NSA_HEADS), rows(HEAD_DIM), rows(HEAD_DIM),
                   pl.BlockSpec((1, keys_per_row_tile, V_AUG, KEY_TILE), lambda i: (0, i, 0, 0)),
                   pl.BlockSpec((IDX_HEADS, ROW_TILE), lambda i: (0, i)), rows(LANES), rows(LANES),
                   heads(G), pl.BlockSpec((G, keys_per_row_tile, V_AUG, KEY_TILE), lambda i: (0, i, 0, 0)),
                   heads(G), pl.BlockSpec((G, win_per_row_tile, HEAD_DIM, WIN_TILE), lambda i: (0, i, 0, 0)),
                   pl.BlockSpec((NSA_HEADS * 3, ROW_TILE), lambda i: (0, i))],
        out_shape=[head_shape(DSA_HEADS), head_shape(IDX_HEADS), head_shape(NSA_HEADS), sds((T, HEAD_DIM), mm),
                   sds((T, HEAD_DIM), mm), sds((1, T // KEY_TILE, V_AUG, KEY_TILE), mm), sds((IDX_HEADS, T), F32),
                   sds((T, LANES), F32), sds((T, LANES), F32),
                   head_shape(G), sds((G, T // KEY_TILE, V_AUG, KEY_TILE), mm),
                   head_shape(G), sds((G, T // WIN_TILE, HEAD_DIM, WIN_TILE), mm),
                   sds((NSA_HEADS * 3, T), F32)],
        compiler_params=_params(1), name="prologue",
    )(x.reshape(T, D), row(norm1_g), w_in_p, row(dsa_cq_g), row(dsa_ckv_g), w_dsa_uq.astype(mm),
      w_dsa_ukv.astype(mm), w_idx_q.astype(mm), ln_g, ln_b, _tile_gain(dsa_qn_g, 512),
      _tile_gain(dsa_kn_g, LANES), _tile_gain(nsa_qn_g, 512), nkg, seg, cos_t, sin_t)

    half = CMP_BLOCK // 2
    eye_g = jnp.eye(G, dtype=F32)

    def cmp_weight(w):
        return jnp.einsum('lde,gh->lgdhe', w, eye_g).reshape(half * G * HEAD_DIM, G * HEAD_DIM)

    def cmp_pe(p):
        return jnp.broadcast_to(p[:, None, :], (half, G, HEAD_DIM)).reshape(1, -1)

    w_cmp = jnp.stack([cmp_weight(w_nsa_cmp[0, :half]), cmp_weight(w_nsa_cmp[0, half:]),
                       cmp_weight(w_nsa_cmp[1, :half]), cmp_weight(w_nsa_cmp[1, half:])]).astype(mm)
    pe_cmp = jnp.concatenate([cmp_pe(nsa_cmp_pe[0, :half]), cmp_pe(nsa_cmp_pe[0, half:]),
                              cmp_pe(nsa_cmp_pe[1, :half]), cmp_pe(nsa_cmp_pe[1, half:])], axis=0)
    cw = half * LANES
    k_c, v_c = pl.pallas_call(
        _compress_kernel,
        grid=(B,),
        in_specs=[pl.BlockSpec((1, n_cmp_pad, cw), lambda b: (b, 0, 0))] * 2
        + [_full((4, cw)), _full((4, cw, LANES)), _full((1, LANES)), _full((LANES, LANES)),
           _full((n_cmp_pad, LANES)), _full((n_cmp_pad, LANES))],
        out_specs=[pl.BlockSpec((1, G, n_cmp_pad, HEAD_DIM), lambda b: (b, 0, 0, 0)),
                   pl.BlockSpec((1, n_cmp_pad, LANES), lambda b: (b, 0, 0))],
        out_shape=[sds((B, G, n_cmp_pad, HEAD_DIM), mm), sds((B, n_cmp_pad, LANES), mm)],
        compiler_params=_params(1), name="compress",
    )(kc_raw.reshape(B, n_cmp_pad, cw), vc_raw.reshape(B, n_cmp_pad, cw), pe_cmp, w_cmp, nkg[0:1], seg[:LANES, :LANES],
      cos_c, sin_c)
    vc_t = v_c.reshape(B, n_cmp_pad, G, HEAD_DIM).transpose(0, 2, 3, 1)

    k_sel = min(DSA_TOPK_MAX, L // 4)
    per_seq = lambda shape: pl.BlockSpec((1,) + shape, lambda b, i: (b,) + (0,) * len(shape))
    seq_heads = pl.BlockSpec((G, L, HEAD_DIM), lambda b, i: (0, b, 0))

    def q_specs(tq):
        n_q = L // tq
        return (pl.BlockSpec((DSA_HEADS, tq, HEAD_DIM), lambda b, i: (0, b * n_q + i, 0)),
                lambda r: pl.BlockSpec((r, tq), lambda b, i: (0, b * n_q + i)),
                pl.BlockSpec((tq, DSA_HEADS * HEAD_DIM), lambda b, i: (b * n_q + i, 0)))

    tq = DSA_Q_TILE
    q_heads, q_cols, out_rows = q_specs(tq)
    o_dsa = pl.pallas_call(
        functools.partial(_dsa_kernel, k_sel=k_sel, idx_bits=int(L - 1).bit_length()),
        grid=(B, L // tq),
        in_specs=[q_heads, q_heads, q_cols(IDX_HEADS),
                  per_seq((L, IDX_DIM)), per_seq((L, HEAD_DIM)),
                  pl.BlockSpec((L // KEY_TILE, V_AUG, KEY_TILE), lambda b, i: (b, 0, 0))],
        out_specs=out_rows,
        out_shape=sds((T, DSA_HEADS * HEAD_DIM), mm),
        scratch_shapes=[pltpu.VMEM((tq // LANES, L, LANES), F32), pltpu.VMEM((2, DSA_HEADS * tq // LANES, KEY_TILE, LANES), F32),
                        pltpu.VMEM((2, DSA_HEADS * tq // LANES, KEY_TILE, LANES), mm),
                        pltpu.VMEM((V_AUG, DSA_HEADS * tq), F32)],
        compiler_params=_params(2), name="dsa",
    )(qi, qd, wi_t, ki.reshape(B, L, IDX_DIM), kd.reshape(B, L, HEAD_DIM), vd_t[0])

    ci = np.arange(n_cmp_pad)[None, :] * CMP_STRIDE
    sj = np.arange(n_slc)[:, None] * SLC_BLOCK
    ov_np = ((ci < sj + SLC_BLOCK) & (ci + CMP_BLOCK > sj) & (np.arange(n_cmp_pad)[None, :] < n_cmp_pad - 1))
    ov_t = jnp.asarray(ov_np.astype(np.float32)).astype(mm)
    tq = NSA_Q_TILE
    q_heads, q_cols, out_rows = q_specs(tq)
    o_nsa = pl.pallas_call(
        functools.partial(_nsa_kernel, n_sel=min(SLC_COUNT, n_slc)),
        grid=(B, L // tq),
        in_specs=[q_heads, per_seq((G, n_cmp_pad, HEAD_DIM)), per_seq((G, HEAD_DIM, n_cmp_pad)),
                  pl.BlockSpec((n_slc, n_cmp_pad), lambda b, i: (0, 0)),
                  seq_heads, pl.BlockSpec((G, L // KEY_TILE, V_AUG, KEY_TILE), lambda b, i: (0, b, 0, 0)),
                  seq_heads, pl.BlockSpec((G, L // WIN_TILE, HEAD_DIM, WIN_TILE), lambda b, i: (0, b, 0, 0)),
                  q_cols(NSA_HEADS * 3)],
        out_specs=out_rows,
        out_shape=sds((T, NSA_HEADS * HEAD_DIM), mm),
        scratch_shapes=[pltpu.VMEM((G, n_slc, tq), F32), pltpu.VMEM((2, NSA_GROUP * tq // LANES, KEY_TILE, LANES), F32),
                        pltpu.VMEM((2, NSA_GROUP * tq // LANES, KEY_TILE, LANES), mm),
                        pltpu.VMEM((V_AUG, NSA_GROUP * tq), F32),
                        pltpu.VMEM((HEAD_DIM, NSA_HEADS * tq), F32)],
        compiler_params=_params(2), name="nsa",
    )(qn, k_c, vc_t, ov_t, ksl, vsl_t, kwn, vwn_t, gate_t)

    x2 = pl.pallas_call(
        _post_kernel,
        grid=(n_rt,),
        in_specs=[rows(D), rows(512), rows(512), _full((D, D)), _full((1, D)), _full((D, xd)),
                  _full((1, XATTN_HEAD_DIM)),
                  pl.BlockSpec((1, M, xd), lambda i: (i // rt_per_seq, 0, 0)),
                  pl.BlockSpec((1, M, xd), lambda i: (i // rt_per_seq, 0, 0)), _full((xd, D))],
        out_specs=rows(D),
        out_shape=sds((T, D), F32),
        compiler_params=_params(1), name="post",
    )(x.reshape(T, D), o_dsa, o_nsa, w_out.astype(mm), row(norm2_g), w_xq.astype(mm), row(xq_norm_g), k_mem, v_mem,
      w_xo.astype(mm))

    x3 = pl.pallas_call(
        _mlp_kernel,
        grid=(T // MLP_ROW_TILE, d_ff // MLP_FF_TILE),
        in_specs=[pl.BlockSpec((MLP_ROW_TILE, D), lambda i, k: (i, 0)), pl.BlockSpec((1, D), lambda i, k: (0, 0)),
                  pl.BlockSpec((D, MLP_FF_TILE), lambda i, k: (0, k)),
                  pl.BlockSpec((MLP_FF_TILE, D), lambda i, k: (k, 0))],
        out_specs=pl.BlockSpec((MLP_ROW_TILE, D), lambda i, k: (i, 0)),
        out_shape=sds((T, D), F32),
        scratch_shapes=[pltpu.VMEM((MLP_ROW_TILE, D), mm)],
        compiler_params=pltpu.CompilerParams(dimension_semantics=("parallel", "arbitrary"),
                                             vmem_limit_bytes=VMEM_LIMIT),
        name="mlp",
    )(x2, row(norm3_g), w_ff_in.astype(mm), w_ff_out.astype(mm))
    return x3.reshape(B, L, D)


def kernel(x, mem, norm1_g, w_in, dsa_cq_g, dsa_ckv_g, w_dsa_uq, w_dsa_ukv, w_idx_q, idx_k_ln_g, idx_k_ln_b, dsa_qn_g, dsa_kn_g, nsa_cmp_pe, w_nsa_cmp, nsa_qn_g, nsa_kn_g, w_out, norm2_g, mem_norm_g, w_xq, w_xk, w_xv, xq_norm_g, xk_norm_g, w_xo, norm3_g, w_ff_in, w_ff_out):
    params = (norm1_g, w_in, dsa_cq_g, dsa_ckv_g, w_dsa_uq, w_dsa_ukv, w_idx_q, idx_k_ln_g, idx_k_ln_b, dsa_qn_g,
              dsa_kn_g, nsa_cmp_pe, w_nsa_cmp, nsa_qn_g, nsa_kn_g, w_out, norm2_g, mem_norm_g, w_xq, w_xk, w_xv,
              xq_norm_g, xk_norm_g, w_xo, norm3_g, w_ff_in, w_ff_out)
    for layer in range(norm1_g.shape[0]):
        x = _layer(x, mem, *(p[layer] for p in params))
    return x
```

```python
import functools
import math

import jax
import jax.numpy as jnp
import numpy as np
from jax import lax
from jax.experimental import pallas as pl
from jax.experimental.pallas import tpu as pltpu

F32 = jnp.float32
I32 = jnp.int32
_MM_DTYPE = jnp.bfloat16

HEAD_DIM = 64
HALF_DIM = HEAD_DIM // 2
ROPE_THETA = 10000.0
EPS = 1e-6
ATTN_SCALE = HEAD_DIM ** -0.5
LOG2E = math.log2(math.e)

DSA_HEADS = 8
DSA_Q_RANK = 256
DSA_KV_RANK = 128
IDX_HEADS = 8
IDX_DIM = 64
IDX_SCALE = IDX_HEADS ** -0.5 * IDX_DIM ** -0.5
DSA_TOPK_MAX = 256

NSA_HEADS = 8
NSA_KV_HEADS = 2
NSA_GROUP = NSA_HEADS // NSA_KV_HEADS
CMP_BLOCK = 32
CMP_STRIDE = 16
SLC_BLOCK = 64
SLC_COUNT = 16
WINDOW = 512
FORCE_BONUS = 1e4

XATTN_HEADS = 4
XATTN_HEAD_DIM = 128

LANES = 128
SUBLANES = 8
INT_MIN = -(2 ** 31)
KEY_LOWEST_FINITE = INT_MIN + 0x00800000
F32_LOWEST = float(np.finfo(np.float32).min)
NEG_BIG = -1e30
VMEM_LIMIT = 56 * 1024 * 1024

ROW_TILE = 1024
MLP_ROW_TILE = 512
MLP_FF_TILE = 4096
DSA_Q_TILE = 256
NSA_Q_TILE = 128
KEY_TILE = 256
WIN_TILE = 128
V_AUG = 80
SEARCH_STRIDE = 4


def _mm(a, b):
    return jnp.dot(a.astype(_MM_DTYPE), b.astype(_MM_DTYPE), preferred_element_type=F32)


def _mm_nt(a, b):
    return lax.dot_general(a.astype(_MM_DTYPE), b.astype(_MM_DTYPE), (((1,), (1,)), ((), ())),
                           preferred_element_type=F32)


def _split(y):
    hi = y.astype(_MM_DTYPE)
    return hi, (y - hi.astype(F32)).astype(_MM_DTYPE)


def _split_mm(y, mat):
    hi, lo = _split(y)
    return jnp.dot(hi, mat, preferred_element_type=F32) + jnp.dot(lo, mat, preferred_element_type=F32)


def _split_mm_left(mat, y):
    hi, lo = _split(y)
    return jnp.dot(mat, hi, preferred_element_type=F32) + jnp.dot(mat, lo, preferred_element_type=F32)


def _rms_rows(x, g):
    return x * lax.rsqrt(jnp.mean(x * x, axis=-1, keepdims=True) + EPS) * g


def _head_rms(x, g, seg):
    return x * lax.rsqrt(_split_mm(x * x, seg) + EPS) * g


def _tile_lanes(t, n):
    return t if n == 1 else jnp.concatenate([t] * n, axis=1)


def _rope(y, cos_t, sin_t):
    w = y.shape[-1]
    lane = lax.broadcasted_iota(I32, y.shape, 1)
    swapped = jnp.where((lane & HALF_DIM) == 0, pltpu.roll(y, w - HALF_DIM, 1), pltpu.roll(y, HALF_DIM, 1))
    n = w // LANES
    return y * _tile_lanes(cos_t, n) + swapped * _tile_lanes(sin_t, n)


def _store_heads(ref, val):
    for hh in range(val.shape[1] // HEAD_DIM):
        ref[hh] = val[:, hh * HEAD_DIM:(hh + 1) * HEAD_DIM].astype(ref.dtype)


def _store_value_tiles(ref, v_t):
    tile = ref.shape[3]
    extra = ref.shape[2] - HEAD_DIM
    if extra:
        ones_row = jnp.where(lax.broadcasted_iota(I32, (extra, tile), 0) == 0, 1.0, 0.0).astype(ref.dtype)
    for g in range(ref.shape[0]):
        for k in range(ref.shape[1]):
            ref[g, k, 0:HEAD_DIM, :] = v_t[g * HEAD_DIM:(g + 1) * HEAD_DIM, k * tile:(k + 1) * tile].astype(ref.dtype)
            if extra:
                ref[g, k, HEAD_DIM:HEAD_DIM + extra, :] = ones_row


def _tree_sum(parts):
    while len(parts) > 1:
        parts = [parts[a] + parts[a + 1] for a in range(0, len(parts) - 1, 2)] + (
            [parts[-1]] if len(parts) % 2 else [])
    return parts[0]


def _memkv_kernel(mem_ref, g_ref, wk_ref, wv_ref, kg_ref, k_ref, v_ref):
    m = _rms_rows(mem_ref[0], g_ref[...])
    k = _mm(m, wk_ref[...])
    v = _mm(m, wv_ref[...])
    ks = []
    for h in range(XATTN_HEADS):
        ks.append(_rms_rows(k[:, h * XATTN_HEAD_DIM:(h + 1) * XATTN_HEAD_DIM], kg_ref[...]))
    k_ref[0] = jnp.concatenate(ks, axis=1).astype(k_ref.dtype)
    v_ref[0] = v.astype(v_ref.dtype)


def _prologue_kernel(x_ref, g1_ref, win_ref, cqg_ref, ckvg_ref, wuq_ref, wukv_ref, widx_ref,
                     lng_ref, lnb_ref, qng_ref, kng_ref, nqg_ref, nkg_ref, seg_ref, cos_ref, sin_ref,
                     qd_ref, qi_ref, qn_ref, ki_ref, kd_ref, vdt_ref, wi_ref, kc_ref, vc_ref,
                     ksl_ref, vslt_ref, kwn_ref, vwnt_ref, gate_ref):
    cos_t = cos_ref[...]
    sin_t = sin_ref[...]
    seg = seg_ref[...]
    seg1 = seg_ref[0:LANES, 0:LANES]
    q_scale = ATTN_SCALE * LOG2E

    h = _rms_rows(x_ref[...], g1_ref[...])
    proj = _mm(h, win_ref[...])

    c_q = _rms_rows(proj[:, 0:256], cqg_ref[...])
    c_kv = _rms_rows(proj[:, 256:384], ckvg_ref[...])

    q_d = _rope(_head_rms(_mm(c_q, wuq_ref[...]), qng_ref[...], seg), cos_t, sin_t) * q_scale
    _store_heads(qd_ref, q_d)
    q_i = _rope(_mm(c_q, widx_ref[...]), cos_t, sin_t)
    _store_heads(qi_ref, q_i)

    kv = _mm(c_kv, wukv_ref[...])
    k_d = _rope(_head_rms(kv, kng_ref[...], seg1), cos_t, sin_t)
    kd_ref[...] = k_d[:, 0:HEAD_DIM].astype(kd_ref.dtype)
    _store_value_tiles(vdt_ref, kv.T[HEAD_DIM:2 * HEAD_DIM, :])

    idx = proj[:, 384:512]
    mu = _split_mm(idx, seg1)
    d = idx - mu
    var = _split_mm(d * d, seg1)
    k_i = _rope(d * lax.rsqrt(var + EPS) * lng_ref[...] + lnb_ref[...], cos_t, sin_t)
    ki_ref[...] = k_i[:, 0:IDX_DIM].astype(ki_ref.dtype)
    wi_ref[...] = (idx * IDX_SCALE).T[IDX_DIM:IDX_DIM + IDX_HEADS, :]

    q_n = _rope(_head_rms(proj[:, 512:1024], nqg_ref[...], seg), cos_t, sin_t) * q_scale
    _store_heads(qn_ref, q_n)

    kc_ref[...] = proj[:, 1024:1152]
    vc_ref[...] = proj[:, 1152:1280]
    _store_heads(ksl_ref, _rope(_head_rms(proj[:, 1280:1408], nkg_ref[1:2, :], seg1), cos_t, sin_t))
    _store_value_tiles(vslt_ref, proj[:, 1408:1536].T)
    _store_heads(kwn_ref, _rope(_head_rms(proj[:, 1536:1664], nkg_ref[2:3, :], seg1), cos_t, sin_t))
    _store_value_tiles(vwnt_ref, proj[:, 1664:1792].T)
    gate_ref[...] = jax.nn.sigmoid(proj[:, 1792:1920]).T[0:gate_ref.shape[0], :]


def _compress_kernel(xk_ref, xv_ref, pe_ref, w_ref, g_ref, seg_ref, cos_ref, sin_ref, kc_ref, vc_ref):
    half = CMP_BLOCK // 2
    n = xk_ref.shape[0] // half

    def cmp_map(x_ref, which):
        a = b = None
        for l in range(half):
            x = x_ref[pl.ds(l, n, stride=half), :]
            cols = slice(l * LANES, (l + 1) * LANES)
            pa = _mm(x + pe_ref[2 * which:2 * which + 1, cols], w_ref[2 * which, cols, :])
            pb = _mm(x + pe_ref[2 * which + 1:2 * which + 2, cols], w_ref[2 * which + 1, cols, :])
            a = pa if a is None else a + pa
            b = pb if b is None else b + pb
        return a + pltpu.roll(b, n - 1, 0)

    k_c = cmp_map(xk_ref, 0)
    k_c = _rope(_head_rms(k_c, g_ref[...], seg_ref[...]), cos_ref[...], sin_ref[...])
    _store_heads(kc_ref.at[0], k_c)
    vc_ref[0] = cmp_map(xv_ref, 1).astype(vc_ref.dtype)


def _flash_pipelined(n_super, scores_fn, v_fn, s_buf, p_buf, acc_ref):
    n_lanes = acc_ref.shape[1]

    def softmax_step(slot, m):
        m_new = jnp.maximum(m, jnp.max(_load_planes(s_buf, slot), axis=0, keepdims=True))
        _store_planes(p_buf, slot, jnp.exp2(_load_planes(s_buf, slot) - m_new))
        return m_new, jnp.exp2(m - m_new)

    def value_step(slot, jj, half, alpha):
        acc_ref[...] = alpha * acc_ref[...] + jnp.dot(v_fn(jj, half), _load_planes(p_buf, slot), preferred_element_type=F32)

    _store_planes(s_buf, 0, scores_fn(0, 0))
    p_buf[1] = jnp.zeros(p_buf.shape[1:], p_buf.dtype)
    acc_ref[...] = jnp.zeros(acc_ref.shape, F32)

    def body(jj, carry, last=False):
        m, alpha = carry
        _store_planes(s_buf, 1, scores_fn(jj, 1))
        m, alpha0 = softmax_step(0, m)
        value_step(1, jnp.maximum(jj - 1, 0), 1, alpha)
        if not last:
            _store_planes(s_buf, 0, scores_fn(jj + 1, 0))
        m, alpha1 = softmax_step(1, m)
        value_step(0, jj, 0, alpha0)
        return m, alpha1

    init = (jnp.full((1, n_lanes), NEG_BIG, F32), jnp.ones((1, n_lanes), F32))
    carry = lax.fori_loop(0, n_super - 1, body, init)
    _, alpha = body(n_super - 1, carry, last=True)
    value_step(1, n_super - 1, 1, alpha)


def _store_planes(ref, slot, val):
    for c in range(ref.shape[1]):
        ref[slot, c] = val[:, c * LANES:(c + 1) * LANES].astype(ref.dtype)


def _load_planes(ref, slot):
    return jnp.concatenate([ref[slot, c] for c in range(ref.shape[1])], axis=1)


def _normalised(acc_ref):
    return acc_ref[0:HEAD_DIM, :] / jnp.maximum(acc_ref[HEAD_DIM:HEAD_DIM + 1, :], 1e-30)


def _softmax_block(s, bias):
    s = s + bias
    mx = jnp.max(s, axis=0, keepdims=True)
    mx = jnp.where(mx > -jnp.inf, mx, 0.0)
    p = jnp.exp2(s - mx)
    return p, jnp.maximum(jnp.sum(p, axis=0, keepdims=True), 1e-30)


def _heads_to_rows(o_t, n_heads, tq):
    stacked = jnp.concatenate([o_t[:, h * tq:(h + 1) * tq] for h in range(n_heads)], axis=0)
    return stacked.T


def _key_to_float(key):
    bits = key ^ ((key >> 31) & jnp.int32(0x7FFFFFFF))
    return lax.bitcast_convert_type(bits, F32)


def _dsa_kernel(qi_ref, qd_ref, wi_ref, ki_ref, kd_ref, vt_ref, o_ref, sc_ref, s_buf, p_buf, acc_ref, *,
                k_sel, idx_bits):
    tq = qi_ref.shape[1]
    tk = vt_ref.shape[2]
    i = pl.program_id(1)
    q0 = i * tq
    n_super = (q0 + tq + 2 * tk - 1) // (2 * tk)
    n_kb = 2 * n_super
    t_row = q0 + lax.broadcasted_iota(I32, (1, tq), 1)
    w = wi_ref[...]
    q_i = qi_ref[...].reshape(IDX_HEADS * tq, IDX_DIM)
    q_d = qd_ref[...].reshape(DSA_HEADS * tq, HEAD_DIM)

    def block_rows(jj, half):
        return pl.multiple_of((2 * jj + half) * tk, tk)

    def load_scores(r0):
        return jnp.concatenate([sc_ref[c, pl.ds(r0, tk), :] for c in range(tq // LANES)], axis=1)

    def store_scores(r0, val):
        for c in range(tq // LANES):
            sc_ref[c, pl.ds(r0, tk), :] = val[:, c * LANES:(c + 1) * LANES]

    def idx_scores(jj, half):
        return _mm_nt(ki_ref[0, pl.ds(block_rows(jj, half), tk), :], q_i)

    def idx_finish(slot, jj, half):
        s = _load_planes(s_buf, slot)
        sc = jnp.maximum(s[:, 0:tq], 0.0) * w[0:1, :]
        for hh in range(1, IDX_HEADS):
            sc = sc + jnp.maximum(s[:, hh * tq:(hh + 1) * tq], 0.0) * w[hh:hh + 1, :]
        r0 = block_rows(jj, half)
        kpos = r0 + lax.broadcasted_iota(I32, (tk, 1), 0)
        store_scores(r0, jnp.where(kpos <= t_row, sc, -jnp.inf))

    _store_planes(s_buf, 0, idx_scores(0, 0))

    def idx_body(jj, carry, last=False):
        _store_planes(s_buf, 1, idx_scores(jj, 1))
        idx_finish(0, jj, 0)
        if not last:
            _store_planes(s_buf, 0, idx_scores(jj + 1, 0))
        idx_finish(1, jj, 1)
        return carry

    lax.fori_loop(0, n_super - 1, idx_body, 0)
    idx_body(n_super - 1, 0, last=True)

    def count(indicator):
        def body(c, cnt):
            for half in range(2):
                r0 = block_rows(c, half)
                rows = r0 + lax.broadcasted_iota(I32, (tk, 1), 0)
                f = indicator(load_scores(r0), rows)
                cnt = cnt + _tree_sum([f[r * SUBLANES:(r + 1) * SUBLANES] for r in range(tk // SUBLANES)])
            return cnt
        cnt = lax.fori_loop(0, n_super, body, jnp.zeros((SUBLANES, tq), F32))
        return jnp.sum(cnt, axis=0, keepdims=True)

    def bit_cond(state):
        return jnp.logical_and(state[0] < 32, state[1] == 0)

    def bit_pass(bi, state):
        thr, n_ge, settled = state
        cand = jnp.where(settled > 0.5, thr, thr + lax.shift_left(jnp.int32(1), 31 - bi))
        cand_f = _key_to_float(cand)
        tot = count(lambda v, rows: jnp.where(v >= cand_f, 1.0, 0.0))
        keep = tot >= k_sel
        return jnp.where(keep, cand, thr), jnp.where(keep, tot, n_ge), jnp.where(tot == k_sel, 1.0, settled)

    def bit_body(state):
        bi, _, thr, n_ge, settled = state
        thr, n_ge, settled = lax.fori_loop(bi, bi + SEARCH_STRIDE, bit_pass, (thr, n_ge, settled))
        return bi + SEARCH_STRIDE, (jnp.min(settled) > 0.5).astype(I32), thr, n_ge, settled

    _, _, thr, n_ge, _ = lax.while_loop(
        bit_cond, bit_body,
        (jnp.int32(0), jnp.int32(0), jnp.full((1, tq), INT_MIN, I32), jnp.zeros((1, tq), F32),
         jnp.zeros((1, tq), F32)))
    few = thr < jnp.int32(KEY_LOWEST_FINITE)
    thr_f = jnp.where(few, F32_LOWEST, _key_to_float(jnp.maximum(thr, jnp.int32(KEY_LOWEST_FINITE))))

    excess = jnp.where(few, 0.0, jnp.where(n_ge > k_sel, 1.0, 0.0))

    @pl.when(jnp.max(excess) > 0.5)
    def _():
        need = k_sel - count(lambda v, rows: jnp.where(v > thr_f, 1.0, 0.0))

        def jbit_body(bi, j0):
            cand = j0 + lax.shift_left(jnp.int32(1), idx_bits - 1 - bi)
            tot = count(lambda v, rows: jnp.where(rows < cand, jnp.where(v == thr_f, 1.0, 0.0), 0.0))
            return jnp.where(tot < need, cand, j0)
        j0 = lax.fori_loop(0, idx_bits, jbit_body, jnp.zeros((1, tq), I32))

        def demote(c, carry):
            r0 = pl.multiple_of(c * tk, tk)
            rows = r0 + lax.broadcasted_iota(I32, (tk, 1), 0)
            v = load_scores(r0)
            demoted = jnp.where(rows > j0, jnp.where(v == thr_f, -jnp.inf, v), v)
            store_scores(r0, jnp.where(excess > 0.5, demoted, v))
            return carry
        lax.fori_loop(0, n_kb, demote, 0)

    def scores_fn(jj, half):
        r0 = block_rows(jj, half)
        sel = jnp.where(load_scores(r0) >= thr_f, 0.0, -jnp.inf)
        return _mm_nt(kd_ref[0, pl.ds(r0, tk), :], q_d) + _tile_lanes(sel, DSA_HEADS)

    _flash_pipelined(n_super, scores_fn, lambda jj, half: vt_ref[2 * jj + half], s_buf, p_buf, acc_ref)
    o_ref[...] = _heads_to_rows(_normalised(acc_ref), DSA_HEADS, tq).astype(o_ref.dtype)


def _nsa_kernel(qn_ref, kc_ref, vct_ref, ov_ref, ksl_ref, vslt_ref, kwn_ref, vwnt_ref, gate_ref, o_ref,
                sel_ref, s_buf, p_buf, acc_ref, out_ref, *, n_sel):
    tq = qn_ref.shape[1]
    n_cmp = kc_ref.shape[2]
    n_slc = ov_ref.shape[0]
    tk = vslt_ref.shape[3]
    tkw = vwnt_ref.shape[3]
    blk_per_tile = tk // SLC_BLOCK
    r_heads = NSA_GROUP
    nq = r_heads * tq
    win_keys = WINDOW + tq
    i = pl.program_id(1)
    q0 = i * tq
    t_row = q0 + lax.broadcasted_iota(I32, (1, tq), 1)
    gates = gate_ref[...]
    q_groups = [qn_ref[g * r_heads:(g + 1) * r_heads].reshape(nq, HEAD_DIM) for g in range(NSA_KV_HEADS)]

    def gate_wide(g, branch):
        return jnp.concatenate(
            [gates[(g * r_heads + r) * 3 + branch:(g * r_heads + r) * 3 + branch + 1, :] for r in range(r_heads)],
            axis=1)

    jrow = lax.broadcasted_iota(I32, (n_slc, 1), 0)
    jrow_f = jrow.astype(F32)
    tb = jnp.right_shift(t_row, SLC_BLOCK.bit_length() - 1)
    bonus = jnp.where(jrow == 0, FORCE_BONUS,
                      jnp.where(jrow == tb, FORCE_BONUS, jnp.where(jrow == tb - 1, FORCE_BONUS, 0.0)))
    importance = []
    for g in range(NSA_KV_HEADS):
        lanes = slice(g * nq, (g + 1) * nq)

        cmp_end = lax.broadcasted_iota(I32, (n_cmp, 1), 0) * CMP_STRIDE + (CMP_BLOCK - 1)
        bias = _tile_lanes(jnp.where(cmp_end <= t_row, 0.0, -jnp.inf), r_heads)
        p, l = _softmax_block(_mm_nt(kc_ref[0, g], q_groups[g]), bias)
        p = p / l
        o_g = _mm(vct_ref[0, g], p) * gate_wide(g, 0)

        p_sum = p[:, 0:tq]
        for r in range(1, r_heads):
            p_sum = p_sum + p[:, r * tq:(r + 1) * tq]
        blk = _split_mm_left(ov_ref[...], p_sum)
        importance.append(jnp.where(jrow * SLC_BLOCK <= t_row, blk + bonus, -jnp.inf))

        start = pl.multiple_of(jnp.maximum(q0 - WINDOW, 0), tkw)
        kpos = start + lax.broadcasted_iota(I32, (win_keys, 1), 0)
        bias = jnp.where(kpos <= t_row, jnp.where(kpos > t_row - WINDOW, 0.0, -jnp.inf), -jnp.inf)
        p, l = _softmax_block(_mm_nt(kwn_ref[g, pl.ds(start, win_keys), :], q_groups[g]),
                              _tile_lanes(bias, r_heads))
        o_w = None
        for k in range(win_keys // tkw):
            part = _mm(vwnt_ref[g, start // tkw + k], p[k * tkw:(k + 1) * tkw, :])
            o_w = part if o_w is None else o_w + part
        out_ref[:, lanes] = o_g + o_w / l * gate_wide(g, 2)

    val = jnp.concatenate(importance, axis=1)
    sel_bias = jnp.full(val.shape, -jnp.inf, F32)
    for _ in range(n_sel):
        top = jnp.max(val, axis=0, keepdims=True)
        first = jnp.min(jnp.where(val == top, jrow_f, float(n_slc)), axis=0, keepdims=True)
        pick = jrow_f == first
        sel_bias = jnp.where(pick, 0.0, sel_bias)
        val = jnp.where(pick, -jnp.inf, val)
    for g in range(NSA_KV_HEADS):
        sel_ref[g] = sel_bias[:, g * tq:(g + 1) * tq]

    n_super = (q0 + tq + 2 * tk - 1) // (2 * tk)
    for g in range(NSA_KV_HEADS):
        lanes = slice(g * nq, (g + 1) * nq)

        def scores_fn(jj, half, g=g):
            r0 = pl.multiple_of((2 * jj + half) * tk, tk)
            picked = sel_ref[g, pl.ds(pl.multiple_of(jj * 2 * blk_per_tile, 2 * blk_per_tile), 2 * blk_per_tile), :]
            blocks = [jnp.broadcast_to(picked[half * blk_per_tile + bb:half * blk_per_tile + bb + 1, :], (SLC_BLOCK, tq))
                      for bb in range(blk_per_tile)]
            kpos = r0 + lax.broadcasted_iota(I32, (tk, 1), 0)
            bias = jnp.where(kpos <= t_row, jnp.concatenate(blocks, axis=0), -jnp.inf)
            return _mm_nt(ksl_ref[g, pl.ds(r0, tk), :], q_groups[g]) + _tile_lanes(bias, r_heads)

        _flash_pipelined(n_super, scores_fn, lambda jj, half, g=g: vslt_ref[g, 2 * jj + half], s_buf, p_buf, acc_ref)
        out_ref[:, lanes] = out_ref[:, lanes] + _normalised(acc_ref) * gate_wide(g, 1)

    o_ref[...] = _heads_to_rows(out_ref[...], NSA_HEADS, tq).astype(o_ref.dtype)


def _post_kernel(x_ref, od_ref, on_ref, wo_ref, g2_ref, wq_ref, qg_ref, km_ref, vm_ref, wxo_ref, o_ref):
    half = od_ref.shape[1]
    x1 = x_ref[...] + _mm(od_ref[...], wo_ref[0:half, :]) + _mm(on_ref[...], wo_ref[half:2 * half, :])
    q = _mm(_rms_rows(x1, g2_ref[...]), wq_ref[...])
    km = km_ref[0]
    vm = vm_ref[0]
    heads = []
    for h in range(XATTN_HEADS):
        lo, hi = h * XATTN_HEAD_DIM, (h + 1) * XATTN_HEAD_DIM
        q_h = _rms_rows(q[:, lo:hi], qg_ref[...])
        s = _mm_nt(q_h, km[:, lo:hi]) * (XATTN_HEAD_DIM ** -0.5)
        p = jnp.exp(s - jnp.max(s, axis=-1, keepdims=True))
        heads.append(_mm(p, vm[:, lo:hi]) / jnp.sum(p, axis=-1, keepdims=True))
    o_ref[...] = x1 + _mm(jnp.concatenate(heads, axis=1), wxo_ref[...])


def _mlp_kernel(x_ref, g3_ref, w1_ref, w2_ref, o_ref, h_ref):
    @pl.when(pl.program_id(1) == 0)
    def _():
        x = x_ref[...]
        h_ref[...] = _rms_rows(x, g3_ref[...]).astype(h_ref.dtype)
        o_ref[...] = x

    a = jnp.maximum(jnp.dot(h_ref[...], w1_ref[...], preferred_element_type=F32), 0.0)
    o_ref[...] += _mm(a * a, w2_ref[...])


def _params(n_axes):
    return pltpu.CompilerParams(dimension_semantics=("parallel",) * n_axes, vmem_limit_bytes=VMEM_LIMIT)


def _full(shape):
    return pl.BlockSpec(shape, lambda *_: (0,) * len(shape))


def _rope_tables(pos):
    inv = ROPE_THETA ** (-jnp.arange(HALF_DIM, dtype=F32) / HALF_DIM)
    ang = pos.astype(F32)[:, None] * inv[None, :]
    cos, sin = jnp.cos(ang), jnp.sin(ang)
    reps = LANES // HEAD_DIM
    return (jnp.tile(jnp.concatenate([cos, cos], axis=1), (1, reps)),
            jnp.tile(jnp.concatenate([-sin, sin], axis=1), (1, reps)))


def _tile_gain(g, width):
    return jnp.tile(g.astype(F32), width // g.shape[0])[None, :]


def _layer(x, mem, norm1_g, w_in, dsa_cq_g, dsa_ckv_g, w_dsa_uq, w_dsa_ukv, w_idx_q, idx_k_ln_g, idx_k_ln_b,
           dsa_qn_g, dsa_kn_g, nsa_cmp_pe, w_nsa_cmp, nsa_qn_g, nsa_kn_g, w_out, norm2_g, mem_norm_g,
           w_xq, w_xk, w_xv, xq_norm_g, xk_norm_g, w_xo, norm3_g, w_ff_in, w_ff_out):
    B, L, D = x.shape
    M = mem.shape[1]
    mm = _MM_DTYPE
    T = B * L
    assert L % (2 * KEY_TILE) == 0 and ROW_TILE % KEY_TILE == 0 and L % ROW_TILE == 0 and T % MLP_ROW_TILE == 0 and L >= WINDOW + NSA_Q_TILE
    n_slc = L // SLC_BLOCK
    n_cmp_pad = L // CMP_STRIDE
    d_ff = w_ff_in.shape[1]
    xd = XATTN_HEADS * XATTN_HEAD_DIM
    G = NSA_KV_HEADS

    z = lambda n: jnp.zeros((D, n), F32)
    w_in_p = jnp.concatenate([
        w_in[:, 0:448], w_in[:, 448:456], z(56),
        w_in[:, 456:968],
        w_in[:, 968:1736],
        w_in[:, 1736:1760], z(104)], axis=1).astype(mm)
    seg_np = np.kron(np.eye(512 // HEAD_DIM, dtype=np.float32), np.full((HEAD_DIM, HEAD_DIM), 1.0 / HEAD_DIM, np.float32))
    seg = jnp.asarray(seg_np).astype(mm)
    cos_t, sin_t = _rope_tables(jnp.arange(L, dtype=jnp.int32))
    cos_c, sin_c = _rope_tables(jnp.arange(n_cmp_pad, dtype=jnp.int32) * CMP_STRIDE + (CMP_BLOCK - 1))
    pad64 = jnp.zeros((HEAD_DIM,), F32)
    ln_g = jnp.concatenate([idx_k_ln_g, pad64])[None, :]
    ln_b = jnp.concatenate([idx_k_ln_b, pad64])[None, :]
    nkg = jnp.tile(nsa_kn_g, (1, LANES // HEAD_DIM))
    row = lambda v: v.astype(F32)[None, :]
    sds = jax.ShapeDtypeStruct

    k_mem, v_mem = pl.pallas_call(
        _memkv_kernel,
        grid=(B,),
        in_specs=[pl.BlockSpec((1, M, D), lambda b: (b, 0, 0)), _full((1, D)), _full((D, xd)), _full((D, xd)),
                  _full((1, XATTN_HEAD_DIM))],
        out_specs=[pl.BlockSpec((1, M, xd), lambda b: (b, 0, 0))] * 2,
        out_shape=[sds((B, M, xd), mm)] * 2,
        compiler_params=_params(1), name="memkv",
    )(mem, row(mem_norm_g), w_xk.astype(mm), w_xv.astype(mm), row(xk_norm_g))

    n_rt = T // ROW_TILE
    rt_per_seq = L // ROW_TILE
    rows = lambda w: pl.BlockSpec((ROW_TILE, w), lambda i: (i, 0))
    heads = lambda n: pl.BlockSpec((n, ROW_TILE, HEAD_DIM), lambda i: (0, i, 0))
    table = pl.BlockSpec((ROW_TILE, LANES), lambda i: (i % rt_per_seq, 0))
    head_shape = lambda n: sds((n, T, HEAD_DIM), mm)
    win_per_row_tile = ROW_TILE // WIN_TILE
    keys_per_row_tile = ROW_TILE // KEY_TILE
    (qd, qi, qn, ki, kd, vd_t, wi_t, kc_raw, vc_raw, ksl, vsl_t, kwn, vwn_t, gate_t) = pl.pallas_call(
        _prologue_kernel,
        grid=(n_rt,),
        in_specs=[rows(D), _full((1, D)), _full(w_in_p.shape), _full((1, DSA_Q_RANK)), _full((1, DSA_KV_RANK)),
                  _full(w_dsa_uq.shape), _full(w_dsa_ukv.shape), _full(w_idx_q.shape),
                  _full((1, LANES)), _full((1, LANES)), _full((1, 512)), _full((1, LANES)), _full((1, 512)),
                  _full((3, LANES)), _full((512, 512)), table, table],
        out_specs=[heads(DSA_HEADS), heads(IDX_HEADS), heads(NSA_HEADS), rows(HEAD_DIM), rows(HEAD_DIM),
                   pl.BlockSpec((1, keys_per_row_tile, V_AUG, KEY_TILE), lambda i: (0, i, 0, 0)),
                   pl.BlockSpec((IDX_HEADS, ROW_TILE), lambda i: (0, i)), rows(LANES), rows(LANES),
                   heads(G), pl.BlockSpec((G, keys_per_row_tile, V_AUG, KEY_TILE), lambda i: (0, i, 0, 0)),
                   heads(G), pl.BlockSpec((G, win_per_row_tile, HEAD_DIM, WIN_TILE), lambda i: (0, i, 0, 0)),
                   pl.BlockSpec((NSA_HEADS * 3, ROW_TILE), lambda i: (0, i))],
        out_shape=[head_shape(DSA_HEADS), head_shape(IDX_HEADS), head_shape(NSA_HEADS), sds((T, HEAD_DIM), mm),
                   sds((T, HEAD_DIM), mm), sds((1, T // KEY_TILE, V_AUG, KEY_TILE), mm), sds((IDX_HEADS, T), F32),
                   sds((T, LANES), F32), sds((T, LANES), F32),
                   head_shape(G), sds((G, T // KEY_TILE, V_AUG, KEY_TILE), mm),
                   head_shape(G), sds((G, T // WIN_TILE, HEAD_DIM, WIN_TILE), mm),
                   sds((NSA_HEADS * 3, T), F32)],
        compiler_params=_params(1), name="prologue",
    )(x.reshape(T, D), row(norm1_g), w_in_p, row(dsa_cq_g), row(dsa_ckv_g), w_dsa_uq.astype(mm),
      w_dsa_ukv.astype(mm), w_idx_q.astype(mm), ln_g, ln_b, _tile_gain(dsa_qn_g, 512),
      _tile_gain(dsa_kn_g, LANES), _tile_gain(nsa_qn_g, 512), nkg, seg, cos_t, sin_t)

    half = CMP_BLOCK // 2
    eye_g = jnp.eye(G, dtype=F32)

    def cmp_weight(w):
        return jnp.einsum('lde,gh->lgdhe', w, eye_g).reshape(half * G * HEAD_DIM, G * HEAD_DIM)

    def cmp_pe(p):
        return jnp.broadcast_to(p[:, None, :], (half, G, HEAD_DIM)).reshape(1, -1)

    w_cmp = jnp.stack([cmp_weight(w_nsa_cmp[0, :half]), cmp_weight(w_nsa_cmp[0, half:]),
                       cmp_weight(w_nsa_cmp[1, :half]), cmp_weight(w_nsa_cmp[1, half:])]).astype(mm)
    pe_cmp = jnp.concatenate([cmp_pe(nsa_cmp_pe[0, :half]), cmp_pe(nsa_cmp_pe[0, half:]),
                              cmp_pe(nsa_cmp_pe[1, :half]), cmp_pe(nsa_cmp_pe[1, half:])], axis=0)
    cw = half * LANES
    k_c, v_c = pl.pallas_call(
        _compress_kernel,
        grid=(B,),
        in_specs=[pl.BlockSpec((L, LANES), lambda b: (b, 0))] * 2
        + [_full((4, cw)), _full((4, cw, LANES)), _full((1, LANES)), _full((LANES, LANES)),
           _full((n_cmp_pad, LANES)), _full((n_cmp_pad, LANES))],
        out_specs=[pl.BlockSpec((1, G, n_cmp_pad, HEAD_DIM), lambda b: (b, 0, 0, 0)),
                   pl.BlockSpec((1, n_cmp_pad, LANES), lambda b: (b, 0, 0))],
        out_shape=[sds((B, G, n_cmp_pad, HEAD_DIM), mm), sds((B, n_cmp_pad, LANES), mm)],
        compiler_params=_params(1), name="compress",
    )(kc_raw, vc_raw, pe_cmp, w_cmp, nkg[0:1], seg[:LANES, :LANES], cos_c, sin_c)
    vc_t = v_c.reshape(B, n_cmp_pad, G, HEAD_DIM).transpose(0, 2, 3, 1)

    k_sel = min(DSA_TOPK_MAX, L // 4)
    per_seq = lambda shape: pl.BlockSpec((1,) + shape, lambda b, i: (b,) + (0,) * len(shape))
    seq_heads = pl.BlockSpec((G, L, HEAD_DIM), lambda b, i: (0, b, 0))

    def q_specs(tq):
        n_q = L // tq
        return (pl.BlockSpec((DSA_HEADS, tq, HEAD_DIM), lambda b, i: (0, b * n_q + i, 0)),
                lambda r: pl.BlockSpec((r, tq), lambda b, i: (0, b * n_q + i)),
                pl.BlockSpec((tq, DSA_HEADS * HEAD_DIM), lambda b, i: (b * n_q + i, 0)))

    tq = DSA_Q_TILE
    q_heads, q_cols, out_rows = q_specs(tq)
    o_dsa = pl.pallas_call(
        functools.partial(_dsa_kernel, k_sel=k_sel, idx_bits=int(L - 1).bit_length()),
        grid=(B, L // tq),
        in_specs=[q_heads, q_heads, q_cols(IDX_HEADS),
                  per_seq((L, IDX_DIM)), per_seq((L, HEAD_DIM)),
                  pl.BlockSpec((L // KEY_TILE, V_AUG, KEY_TILE), lambda b, i: (b, 0, 0))],
        out_specs=out_rows,
        out_shape=sds((T, DSA_HEADS * HEAD_DIM), mm),
        scratch_shapes=[pltpu.VMEM((tq // LANES, L, LANES), F32), pltpu.VMEM((2, DSA_HEADS * tq // LANES, KEY_TILE, LANES), F32),
                        pltpu.VMEM((2, DSA_HEADS * tq // LANES, KEY_TILE, LANES), mm),
                        pltpu.VMEM((V_AUG, DSA_HEADS * tq), F32)],
        compiler_params=_params(2), name="dsa",
    )(qi, qd, wi_t, ki.reshape(B, L, IDX_DIM), kd.reshape(B, L, HEAD_DIM), vd_t[0])

    ci = np.arange(n_cmp_pad)[None, :] * CMP_STRIDE
    sj = np.arange(n_slc)[:, None] * SLC_BLOCK
    ov_np = ((ci < sj + SLC_BLOCK) & (ci + CMP_BLOCK > sj) & (np.arange(n_cmp_pad)[None, :] < n_cmp_pad - 1))
    ov_t = jnp.asarray(ov_np.astype(np.float32)).astype(mm)
    tq = NSA_Q_TILE
    q_heads, q_cols, out_rows = q_specs(tq)
    o_nsa = pl.pallas_call(
        functools.partial(_nsa_kernel, n_sel=min(SLC_COUNT, n_slc)),
        grid=(B, L // tq),
        in_specs=[q_heads, per_seq((G, n_cmp_pad, HEAD_DIM)), per_seq((G, HEAD_DIM, n_cmp_pad)),
                  pl.BlockSpec((n_slc, n_cmp_pad), lambda b, i: (0, 0)),
                  seq_heads, pl.BlockSpec((G, L // KEY_TILE, V_AUG, KEY_TILE), lambda b, i: (0, b, 0, 0)),
                  seq_heads, pl.BlockSpec((G, L // WIN_TILE, HEAD_DIM, WIN_TILE), lambda b, i: (0, b, 0, 0)),
                  q_cols(NSA_HEADS * 3)],
        out_specs=out_rows,
        out_shape=sds((T, NSA_HEADS * HEAD_DIM), mm),
        scratch_shapes=[pltpu.VMEM((G, n_slc, tq), F32), pltpu.VMEM((2, NSA_GROUP * tq // LANES, KEY_TILE, LANES), F32),
                        pltpu.VMEM((2, NSA_GROUP * tq // LANES, KEY_TILE, LANES), mm),
                        pltpu.VMEM((V_AUG, NSA_GROUP * tq), F32),
                        pltpu.VMEM((HEAD_DIM, NSA_HEADS * tq), F32)],
        compiler_params=_params(2), name="nsa",
    )(qn, k_c, vc_t, ov_t, ksl, vsl_t, kwn, vwn_t, gate_t)

    x2 = pl.pallas_call(
        _post_kernel,
        grid=(n_rt,),
        in_specs=[rows(D), rows(512), rows(512), _full((D, D)), _full((1, D)), _full((D, xd)),
                  _full((1, XATTN_HEAD_DIM)),
                  pl.BlockSpec((1, M, xd), lambda i: (i // rt_per_seq, 0, 0)),
                  pl.BlockSpec((1, M, xd), lambda i: (i // rt_per_seq, 0, 0)), _full((xd, D))],
        out_specs=rows(D),
        out_shape=sds((T, D), F32),
        compiler_params=_params(1), name="post",
    )(x.reshape(T, D), o_dsa, o_nsa, w_out.astype(mm), row(norm2_g), w_xq.astype(mm), row(xq_norm_g), k_mem, v_mem,
      w_xo.astype(mm))

    x3 = pl.pallas_call(
        _mlp_kernel,
        grid=(T // MLP_ROW_TILE, d_ff // MLP_FF_TILE),
        in_specs=[pl.BlockSpec((MLP_ROW_TILE, D), lambda i, k: (i, 0)), pl.BlockSpec((1, D), lambda i, k: (0, 0)),
                  pl.BlockSpec((D, MLP_FF_TILE), lambda i, k: (0, k)),
                  pl.BlockSpec((MLP_FF_TILE, D), lambda i, k: (k, 0))],
        out_specs=pl.BlockSpec((MLP_ROW_TILE, D), lambda i, k: (i, 0)),
        out_shape=sds((T, D), F32),
        scratch_shapes=[pltpu.VMEM((MLP_ROW_TILE, D), mm)],
        compiler_params=pltpu.CompilerParams(dimension_semantics=("parallel", "arbitrary"),
                                             vmem_limit_bytes=VMEM_LIMIT),
        name="mlp",
    )(x2, row(norm3_g), w_ff_in.astype(mm), w_ff_out.astype(mm))
    return x3.reshape(B, L, D)


def kernel(x, mem, norm1_g, w_in, dsa_cq_g, dsa_ckv_g, w_dsa_uq, w_dsa_ukv, w_idx_q, idx_k_ln_g, idx_k_ln_b, dsa_qn_g, dsa_kn_g, nsa_cmp_pe, w_nsa_cmp, nsa_qn_g, nsa_kn_g, w_out, norm2_g, mem_norm_g, w_xq, w_xk, w_xv, xq_norm_g, xk_norm_g, w_xo, norm3_g, w_ff_in, w_ff_out):
    params = (norm1_g, w_in, dsa_cq_g, dsa_ckv_g, w_dsa_uq, w_dsa_ukv, w_idx_q, idx_k_ln_g, idx_k_ln_b, dsa_qn_g,
              dsa_kn_g, nsa_cmp_pe, w_nsa_cmp, nsa_qn_g, nsa_kn_g, w_out, norm2_g, mem_norm_g, w_xq, w_xk, w_xv,
              xq_norm_g, xk_norm_g, w_xo, norm3_g, w_ff_in, w_ff_out)
    for layer in range(norm1_g.shape[0]):
        x = _layer(x, mem, *(p[layer] for p in params))
    return x
```
